```python
import math
import jax, jax.numpy as jnp
from jax import lax
import numpy as np


D_MODEL = 1024
BATCH = 8
SEQ = 4096
DEPTH = 4

N_MEM = 256
N_BRANCH = 4
W_BRANCH = D_MODEL // 2
POOL_WINDOWS = (2, 4, 8, 16)
POOL_GROUP = W_BRANCH // len(POOL_WINDOWS)
DIL_GROUPS = ((128, 1), (512, 4), (2048, 16))
ATT_HEADS = 8
ATT_HEAD_DIM = W_BRANCH // ATT_HEADS
SSM_GROUP = 16
SSM_GROUPS = W_BRANCH // SSM_GROUP
SSM_STATE = 64
SGU_CHUNK = 128
SGU_GROUPS = 4
SGU_GROUP_DIM = W_BRANCH // SGU_GROUPS
X_HEADS = 4
X_HEAD_DIM = 128
D_FF = 4 * D_MODEL
REL_BUCKETS = 32
REL_MAX_DIST = 2048
EPS = 1e-6
NEG_INF = -1e30
N_ATT_COLS = 3 * len(DIL_GROUPS) * W_BRANCH
OFF_POOL = 0
OFF_ATT = OFF_POOL + W_BRANCH
OFF_SSM = OFF_ATT + N_ATT_COLS
OFF_SGU = OFF_SSM + W_BRANCH
OFF_GATE = OFF_SGU + 2 * W_BRANCH
IN_WIDTH = OFF_GATE + N_BRANCH * D_MODEL

kernel_name = 'hybrid_gated_parallel_mixer_block'


def rmsnorm(x, g):
    xf = x.astype(jnp.float32)
    y = xf * lax.rsqrt(jnp.mean(xf * xf, axis=-1, keepdims=True) + EPS)
    return (y * g.astype(jnp.float32)).astype(x.dtype)


def _t5_bucket(n):
    exact = REL_BUCKETS // 2
    nf = np.maximum(n, 1).astype(np.float32)
    large = exact + (np.log(nf / exact) / np.log(REL_MAX_DIST / exact) * (REL_BUCKETS - exact)).astype(np.int32)
    large = np.minimum(large, REL_BUCKETS - 1)
    return np.where(n < exact, n, large).astype(np.int32)


def _band_pattern(band, dil):
    i = np.arange(band)[:, None]
    kk = np.arange(2 * band)[None, :]
    dist = band + i - kk
    local = (dist >= 0) & (dist <= band)
    bucket = _t5_bucket(np.clip(dist, 0, band) * dil)
    return local, bucket


def pool_mixer(h, w_pool, scale):
    B, S, _ = h.shape
    hf = h.astype(jnp.float32)
    cs = jnp.pad(jnp.cumsum(hf, axis=1), ((0, 0), (1, 0), (0, 0)))
    t = jnp.arange(S)
    outs = []
    for gi, w in enumerate(POOL_WINDOWS):
        sl = slice(gi * POOL_GROUP, (gi + 1) * POOL_GROUP)
        lo = jnp.maximum(t + 1 - w, 0)
        cnt = jnp.minimum(t + 1, w).astype(jnp.float32)
        mean = (cs[:, 1:, sl] - cs[:, lo, sl]) / cnt[None, :, None]
        outs.append(mean - hf[..., sl])
    p = jnp.stack(outs, axis=2)
    y = jnp.einsum('bsgc,gcd->bsgd', p, w_pool.astype(jnp.float32)).reshape(B, S, W_BRANCH)
    return (y * scale.astype(jnp.float32)).astype(h.dtype)


def _dilated_group(q, k, v, bias, local, band, dil):
    B, S, H, E = q.shape
    L = S // dil
    nb = -(-L // band)
    Lp = nb * band

    def to_sub(a):
        a = a.reshape(B, L, dil, H, E).transpose(0, 2, 1, 3, 4)
        a = jnp.pad(a, ((0, 0), (0, 0), (0, Lp - L), (0, 0), (0, 0)))
        return a.reshape(B, dil, nb, band, H, E)

    def with_prev(a):
        prev = jnp.pad(a, ((0, 0), (0, 0), (1, 0), (0, 0), (0, 0), (0, 0)))[:, :, :-1]
        return jnp.concatenate([prev, a], axis=3)

    qs = to_sub(q)
    kk = with_prev(to_sub(k))
    vv = with_prev(to_sub(v))
    logits = jnp.einsum('brnqhe,brnkhe->brnhqk', qs, kk).astype(jnp.float32) * (E ** -0.5) + bias
    first = (jnp.arange(nb) == 0)[:, None, None] & (jnp.arange(2 * band) < band)[None, None, :]
    valid = local[None] & ~first
    logits = jnp.where(valid[None, None, :, None], logits, NEG_INF)
    m = jnp.max(logits, axis=-1, keepdims=True)
    p = jnp.exp(logits - m)
    s = jnp.sum(p, axis=-1, keepdims=True)
    o = jnp.einsum('brnhqk,brnkhe->brnqhe', (p / s).astype(v.dtype), vv)
    lse = (m + jnp.log(s))[..., 0]
    o = o.reshape(B, dil, Lp, H, E)[:, :, :L].transpose(0, 2, 1, 3, 4).reshape(B, S, H, E)
    lse = lse.transpose(0, 1, 2, 4, 3).reshape(B, dil, Lp, H)[:, :, :L]
    lse = lse.transpose(0, 2, 1, 3).reshape(B, S, H)
    return o, lse


def dilated_attention(qkv, rel_bias):
    B, S, _ = qkv.shape
    ng = len(DIL_GROUPS)
    q, k, v = [a.reshape(B, S, ng, ATT_HEADS, ATT_HEAD_DIM) for a in jnp.split(qkv, 3, axis=-1)]
    outs, lses = [], []
    for g, (win, dil) in enumerate(DIL_GROUPS):
        band = win // dil
        local, bucket = _band_pattern(band, dil)
        bias = rel_bias[jnp.asarray(bucket)][..., g * ATT_HEADS:(g + 1) * ATT_HEADS]
        bias = bias.transpose(2, 0, 1).astype(jnp.float32)
        o, lse = _dilated_group(q[:, :, g], k[:, :, g], v[:, :, g], bias, jnp.asarray(local), band, dil)
        outs.append(o.astype(jnp.float32))
        lses.append(lse)
    wts = jax.nn.softmax(jnp.stack(lses, axis=0), axis=0)
    out = jnp.sum(wts[..., None] * jnp.stack(outs, axis=0), axis=0)
    return out.reshape(B, S, W_BRANCH).astype(qkv.dtype)


def s5_mixer(u, a_re, a_im, log_dt, b_re, b_im, c_re, c_im, d_skip, w_glu, b_glu):
    f32 = jnp.float32
    B, S, _ = u.shape
    uf = u.astype(f32).reshape(B, S, SSM_GROUPS, SSM_GROUP)
    lam_re = jnp.minimum(a_re.astype(f32), -1e-4)
    lam_im = a_im.astype(f32)
    dt = jnp.exp(log_dt.astype(f32))[:, None]
    mag = jnp.exp(lam_re * dt)
    ab_re, ab_im = mag * jnp.cos(lam_im * dt), mag * jnp.sin(lam_im * dt)
    den = lam_re * lam_re + lam_im * lam_im
    f_re = ((ab_re - 1.0) * lam_re + ab_im * lam_im) / den
    f_im = (ab_im * lam_re - (ab_re - 1.0) * lam_im) / den
    br, bi = b_re.astype(f32), b_im.astype(f32)
    bb_re = f_re[..., None] * br - f_im[..., None] * bi
    bb_im = f_re[..., None] * bi + f_im[..., None] * br
    bu_re = jnp.einsum('bsgc,gpc->bsgp', uf, bb_re)
    bu_im = jnp.einsum('bsgc,gpc->bsgp', uf, bb_im)

    def combine(e1, e2):
        a1r, a1i, b1r, b1i = e1
        a2r, a2i, b2r, b2i = e2
        return (a2r * a1r - a2i * a1i, a2r * a1i + a2i * a1r,
                a2r * b1r - a2i * b1i + b2r, a2r * b1i + a2i * b1r + b2i)

    ar = jnp.broadcast_to(ab_re[None, None], (1, S, SSM_GROUPS, SSM_STATE))
    ai = jnp.broadcast_to(ab_im[None, None], (1, S, SSM_GROUPS, SSM_STATE))
    _, _, hr, hi = lax.associative_scan(combine, (ar, ai, bu_re, bu_im), axis=1)
    y = (jnp.einsum('bsgp,gcp->bsgc', hr, c_re.astype(f32))
         - jnp.einsum('bsgp,gcp->bsgc', hi, c_im.astype(f32))
         + uf * d_skip.astype(f32).reshape(SSM_GROUPS, SSM_GROUP))
    g = jax.nn.gelu(y.reshape(B, S, W_BRANCH))
    out = g * jax.nn.sigmoid(g @ w_glu.astype(f32) + b_glu.astype(f32))
    return out.astype(u.dtype)


def sgu_mixer(z, ln_g, ln_b, w_s, b_s):
    B, S, _ = z.shape
    z = jax.nn.gelu(z)
    u, v = jnp.split(z, 2, axis=-1)
    vf = v.astype(jnp.float32)
    mu = jnp.mean(vf, axis=-1, keepdims=True)
    var = jnp.mean(jnp.square(vf - mu), axis=-1, keepdims=True)
    vf = (vf - mu) * lax.rsqrt(var + EPS) * ln_g.astype(jnp.float32) + ln_b.astype(jnp.float32)
    vf = vf.reshape(B, S // SGU_CHUNK, SGU_CHUNK, SGU_GROUPS, SGU_GROUP_DIM)
    tri = jnp.tril(jnp.ones((SGU_CHUNK, SGU_CHUNK), jnp.float32))
    ws = w_s.astype(jnp.float32) * tri[None]
    sv = jnp.einsum('gts,bnsgc->bntgc', ws, vf) + b_s.astype(jnp.float32).T[:, :, None]
    return (u.astype(jnp.float32) * sv.reshape(B, S, W_BRANCH)).astype(z.dtype)


def cross_attn(h, mem_n, w_cq, w_ckv, w_co):
    B, S, _ = h.shape
    q = (h @ w_cq).reshape(B, S, X_HEADS, X_HEAD_DIM)
    k, v = jnp.split(mem_n @ w_ckv, 2, axis=-1)
    k = k.reshape(B, N_MEM, X_HEADS, X_HEAD_DIM)
    v = v.reshape(B, N_MEM, X_HEADS, X_HEAD_DIM)
    logits = jnp.einsum('bshe,bmhe->bhsm', q, k).astype(jnp.float32) * (X_HEAD_DIM ** -0.5)
    p = jax.nn.softmax(logits, axis=-1).astype(h.dtype)
    o = jnp.einsum('bhsm,bmhe->bshe', p, v).reshape(B, S, X_HEADS * X_HEAD_DIM)
    return o @ w_co


def setup_inputs(seed: int = 0) -> dict:
    key = jax.random.key(seed)
    ks = iter(jax.random.split(key, 48))
    f32 = jnp.float32
    L = DEPTH

    def nrm(shape, scale):
        return jax.random.normal(next(ks), shape, f32) * scale

    def gain(shape):
        return 1.0 + nrm(shape, 0.02)

    d = {}
    d['x'] = nrm((BATCH, SEQ, D_MODEL), 1.0)
    d['mem'] = nrm((BATCH, N_MEM, D_MODEL), 1.0)
    d['rel_bias'] = nrm((REL_BUCKETS, len(DIL_GROUPS) * ATT_HEADS), 0.5)
    d['g_mix_pre'] = gain((L, D_MODEL))
    d['g_mix_post'] = gain((L, D_MODEL))
    d['w_in'] = nrm((L, D_MODEL, IN_WIDTH), D_MODEL ** -0.5)
    d['gate_b'] = nrm((L, N_BRANCH, D_MODEL), 0.01)
    d['pool_w'] = nrm((L, len(POOL_WINDOWS), POOL_GROUP, POOL_GROUP), POOL_GROUP ** -0.5)
    d['pool_scale'] = 1.0 + nrm((L, W_BRANCH), 0.1)
    n_idx = jnp.arange(SSM_STATE, dtype=f32)
    d['a_re'] = -0.5 + nrm((L, SSM_GROUPS, SSM_STATE), 0.01)
    d['a_im'] = jnp.pi * n_idx + nrm((L, SSM_GROUPS, SSM_STATE), 0.01)
    d['log_dt'] = jax.random.uniform(next(ks), (L, SSM_GROUPS), f32, math.log(1e-3), math.log(1e-1))
    d['b_re'] = nrm((L, SSM_GROUPS, SSM_STATE, SSM_GROUP), (2 * SSM_GROUP) ** -0.5)
    d['b_im'] = nrm((L, SSM_GROUPS, SSM_STATE, SSM_GROUP), (2 * SSM_GROUP) ** -0.5)
    d['c_re'] = nrm((L, SSM_GROUPS, SSM_GROUP, SSM_STATE), (2 * SSM_STATE) ** -0.5)
    d['c_im'] = nrm((L, SSM_GROUPS, SSM_GROUP, SSM_STATE), (2 * SSM_STATE) ** -0.5)
    d['d_skip'] = nrm((L, W_BRANCH), 1.0)
    d['w_glu'] = nrm((L, W_BRANCH, W_BRANCH), W_BRANCH ** -0.5)
    d['b_glu'] = nrm((L, W_BRANCH), 0.01)
    d['sgu_ln_g'] = gain((L, W_BRANCH))
    d['sgu_ln_b'] = nrm((L, W_BRANCH), 0.01)
    d['w_s'] = nrm((L, SGU_GROUPS, SGU_CHUNK, SGU_CHUNK), SGU_CHUNK ** -0.5)
    d['b_s'] = 1.0 + nrm((L, SGU_GROUPS, SGU_CHUNK), 0.01)
    d['w_up'] = nrm((L, N_BRANCH, W_BRANCH, D_MODEL), W_BRANCH ** -0.5)
    d['w_out'] = nrm((L, D_MODEL, D_MODEL), D_MODEL ** -0.5)
    d['g_x_pre'] = gain((L, D_MODEL))
    d['g_x_post'] = gain((L, D_MODEL))
    d['g_mem'] = gain((L, D_MODEL))
    d['w_cq'] = nrm((L, D_MODEL, X_HEADS * X_HEAD_DIM), D_MODEL ** -0.5)
    d['w_ckv'] = nrm((L, D_MODEL, 2 * X_HEADS * X_HEAD_DIM), D_MODEL ** -0.5)
    d['w_co'] = nrm((L, X_HEADS * X_HEAD_DIM, D_MODEL), (X_HEADS * X_HEAD_DIM) ** -0.5)
    d['g_ff_pre'] = gain((L, D_MODEL))
    d['g_ff_post'] = gain((L, D_MODEL))
    d['w_ff1'] = nrm((L, D_MODEL, D_FF), D_MODEL ** -0.5)
    d['w_ff2'] = nrm((L, D_FF, D_MODEL), D_FF ** -0.5)
    return d


def reference(x, mem, rel_bias, g_mix_pre, g_mix_post, w_in, gate_b, pool_w, pool_scale,
              a_re, a_im, log_dt, b_re, b_im, c_re, c_im, d_skip, w_glu, b_glu,
              sgu_ln_g, sgu_ln_b, w_s, b_s, w_up, w_out, g_x_pre, g_x_post, g_mem,
              w_cq, w_ckv, w_co, g_ff_pre, g_ff_post, w_ff1, w_ff2):
    B, S, _ = x.shape
    for l in range(DEPTH):
        h = rmsnorm(x, g_mix_pre[l])
        proj = h @ w_in[l]
        a_out = pool_mixer(proj[..., OFF_POOL:OFF_ATT], pool_w[l], pool_scale[l])
        b_out = dilated_attention(proj[..., OFF_ATT:OFF_SSM], rel_bias)
        c_out = s5_mixer(proj[..., OFF_SSM:OFF_SGU], a_re[l], a_im[l], log_dt[l], b_re[l], b_im[l],
                         c_re[l], c_im[l], d_skip[l], w_glu[l], b_glu[l])
        d_out = sgu_mixer(proj[..., OFF_SGU:OFF_GATE], sgu_ln_g[l], sgu_ln_b[l], w_s[l], b_s[l])
        gates = jax.nn.sigmoid(proj[..., OFF_GATE:].reshape(B, S, N_BRANCH, D_MODEL) + gate_b[l])
        branches = (a_out, b_out, c_out, d_out)
        merged = gates[:, :, 0] * (branches[0] @ w_up[l, 0])
        for i in range(1, N_BRANCH):
            merged = merged + gates[:, :, i] * (branches[i] @ w_up[l, i])
        x = x + rmsnorm(merged @ w_out[l], g_mix_post[l])
        h = rmsnorm(x, g_x_pre[l])
        mem_n = rmsnorm(mem, g_mem[l])
        x = x + rmsnorm(cross_attn(h, mem_n, w_cq[l], w_ckv[l], w_co[l]), g_x_post[l])
        h = rmsnorm(x, g_ff_pre[l])
        ff = jnp.square(jax.nn.relu(h @ w_ff1[l])) @ w_ff2[l]
        x = x + rmsnorm(ff, g_ff_post[l])
    return x
```

```python
import functools
import math

import jax
import jax.numpy as jnp
import numpy as np
from jax import lax
from jax.experimental import pallas as pl
from jax.experimental.pallas import tpu as pltpu

F32 = jnp.float32
BF16 = jnp.bfloat16

D_MODEL = 1024
N_MEM = 256
N_BRANCH = 4
W_BRANCH = D_MODEL // 2
POOL_WINDOWS = (2, 4, 8, 16)
POOL_GROUP = W_BRANCH // len(POOL_WINDOWS)
DIL_GROUPS = ((128, 1), (512, 4), (2048, 16))
ATT_HEADS = 8
ATT_HEAD_DIM = W_BRANCH // ATT_HEADS
SSM_GROUP = 16
SSM_GROUPS = W_BRANCH // SSM_GROUP
SSM_STATE = 64
SSM_STATES = SSM_GROUPS * SSM_STATE
SGU_CHUNK = 128
SGU_GROUPS = 4
X_HEADS = 4
X_HEAD_DIM = 128
D_FF = 4 * D_MODEL
REL_BUCKETS = 32
REL_MAX_DIST = 2048
EPS = 1e-6
NEG_INF = -1e30
N_ATT_COLS = 3 * len(DIL_GROUPS) * W_BRANCH
OFF_GATE = W_BRANCH + N_ATT_COLS + W_BRANCH + 2 * W_BRANCH

V7X_LANES = 128
V7X_SUBLANES = 8
V7X_MXU_DIM = 256
V7X_VMEM_LIMIT_BYTES = 56 * 1024 * 1024

TOKEN_BLOCK = 512
POOL_HALO = 16
ATT_BAND = 128
S5_STEPS = 64
S5_LANE_CHUNK = 512
FF_CHUNK = 512


def _params(*sem):
    return pltpu.CompilerParams(dimension_semantics=sem, vmem_limit_bytes=V7X_VMEM_LIMIT_BYTES)


def _const_spec(shape):
    nd = len(shape)
    return pl.BlockSpec(shape, lambda *_: (0,) * nd, pipeline_mode=pl.Buffered(1))


def _rms(xf, g):
    return xf * lax.rsqrt(jnp.mean(xf * xf, axis=-1, keepdims=True) + EPS) * g


def _gelu(x):
    c = math.sqrt(2.0 / math.pi)
    return 0.5 * x * (1.0 + jnp.tanh(c * (x + 0.044715 * (x * x * x))))


def _sigmoid(x):
    return 1.0 / (1.0 + jnp.exp(-x))


def _dot(a, b):
    return jnp.dot(a, b, preferred_element_type=F32)


def _in_proj_kernel(x_ref, g_ref, w_ref, pool_ref, qkv_ref, ssm_ref, sgu_ref):
    h = _rms(x_ref[...], g_ref[...]).astype(BF16)
    step = W_BRANCH
    pool_ref[...] = _dot(h, w_ref[:, 0:step]).astype(BF16)
    for j in range(N_ATT_COLS // step):
        lo = W_BRANCH + j * step
        qkv_ref[:, j * step:(j + 1) * step] = _dot(h, w_ref[:, lo:lo + step]).astype(BF16)
    lo = W_BRANCH + N_ATT_COLS
    ssm_ref[...] = _dot(h, w_ref[:, lo:lo + step]).astype(BF16)
    lo += W_BRANCH
    for j in range(2):
        sgu_ref[:, j * step:(j + 1) * step] = _dot(h, w_ref[:, lo + j * step:lo + (j + 1) * step]).astype(BF16)


def _in_proj(x, g, w):
    B, S, D = x.shape
    tm = TOKEN_BLOCK
    return pl.pallas_call(
        _in_proj_kernel,
        grid=(B, S // tm),
        in_specs=[
            pl.BlockSpec((None, tm, D), lambda b, i: (b, i, 0)),
            _const_spec((1, D)),
            _const_spec((D, OFF_GATE)),
        ],
        out_specs=[
            pl.BlockSpec((None, tm, W_BRANCH), lambda b, i: (b, i, 0)),
            pl.BlockSpec((None, tm, N_ATT_COLS), lambda b, i: (b, i, 0)),
            pl.BlockSpec((tm, W_BRANCH), lambda b, i: (i, b)),
            pl.BlockSpec((None, tm, 2 * W_BRANCH), lambda b, i: (b, i, 0)),
        ],
        out_shape=[
            jax.ShapeDtypeStruct((B, S, W_BRANCH), BF16),
            jax.ShapeDtypeStruct((B, S, N_ATT_COLS), BF16),
            jax.ShapeDtypeStruct((S, B * W_BRANCH), BF16),
            jax.ShapeDtypeStruct((B, S, 2 * W_BRANCH), BF16),
        ],
        compiler_params=_params("parallel", "parallel"),
        name="in_proj",
    )(x, g, w)


def _pool_kernel(cur_ref, halo_ref, w_ref, scale_ref, out_ref, ext_ref):
    i = pl.program_id(1)
    tb = cur_ref.shape[0]
    halo = halo_ref[...].astype(F32)
    ext_ref[0:POOL_HALO, :] = jnp.where(i == 0, 0.0, halo)
    ext_ref[POOL_HALO:, :] = cur_ref[...].astype(F32)
    tpos = i * tb + lax.broadcasted_iota(jnp.int32, (tb, 1), 0)
    for gi, win in enumerate(POOL_WINDOWS):
        lanes = slice(gi * POOL_GROUP, (gi + 1) * POOL_GROUP)
        tok = ext_ref[POOL_HALO:POOL_HALO + tb, lanes]
        s = tok
        for j in range(1, win):
            s = s + ext_ref[POOL_HALO - j:POOL_HALO - j + tb, lanes]
        cnt = jnp.minimum(tpos + 1, win).astype(F32)
        p = s / cnt - tok
        y = _dot(p.astype(BF16), w_ref[gi])
        out_ref[:, lanes] = (y * scale_ref[:, lanes]).astype(BF16)


def _pool(pool_in, w_pool, scale):
    B, S, W = pool_in.shape
    tb = TOKEN_BLOCK
    hb = tb // POOL_HALO
    return pl.pallas_call(
        _pool_kernel,
        grid=(B, S // tb),
        in_specs=[
            pl.BlockSpec((None, tb, W), lambda b, i: (b, i, 0)),
            pl.BlockSpec((None, POOL_HALO, W), lambda b, i: (b, jnp.maximum(i * hb - 1, 0), 0)),
            _const_spec(w_pool.shape),
            _const_spec((1, W)),
        ],
        out_specs=pl.BlockSpec((None, tb, W), lambda b, i: (b, i, 0)),
        out_shape=jax.ShapeDtypeStruct((B, S, W), BF16),
        scratch_shapes=[pltpu.VMEM((POOL_HALO + tb, W), F32)],
        compiler_params=_params("parallel", "parallel"),
        name="pool",
    )(pool_in, pool_in, w_pool, scale)


def _attn_kernel(q_ref, kp_ref, kc_ref, vp_ref, vc_ref, bias_ref, o_ref, lse_ref):
    n = pl.program_id(2)
    band = ATT_BAND
    q = q_ref[...]
    k = jnp.concatenate([kp_ref[...], kc_ref[...]], axis=0)
    v = jnp.concatenate([vp_ref[...], vc_ref[...]], axis=0)
    lane = lax.broadcasted_iota(jnp.int32, (1, V7X_LANES), 1)
    low = lane < ATT_HEAD_DIM
    no_prev = jnp.logical_and(n == 0, lax.broadcasted_iota(jnp.int32, (1, 2 * band), 1) < band)
    scale = ATT_HEAD_DIM ** -0.5
    lse_tile = jnp.zeros((band, V7X_LANES), F32)
    for j in range(ATT_HEADS // 2):
        cols = slice(j * V7X_LANES, (j + 1) * V7X_LANES)
        qp, kp, vp = q[:, cols], k[:, cols], v[:, cols]
        o_pair = jnp.zeros((band, V7X_LANES), F32)
        for e in range(2):
            sel = low if e == 0 else jnp.logical_not(low)
            qh = jnp.where(sel, qp, jnp.zeros_like(qp))
            s = lax.dot_general(qh, kp, (((1,), (1,)), ((), ())), preferred_element_type=F32)
            s = s * scale + bias_ref[2 * j + e]
            s = jnp.where(no_prev, NEG_INF, s)
            m = jnp.max(s, axis=-1, keepdims=True)
            p = jnp.exp(s - m)
            l = jnp.sum(p, axis=-1, keepdims=True)
            vh = jnp.where(sel, vp, jnp.zeros_like(vp))
            o_pair = o_pair + _dot(p.astype(BF16), vh) * (1.0 / l)
            lse_tile = jnp.where(lane == 2 * j + e, m + jnp.log(l), lse_tile)
        o_ref[:, cols] = o_pair.astype(BF16)
    lse_ref[...] = lse_tile


def _attn_group(qkv, bias, g, dil):
    B, S, C = qkv.shape
    L = S // dil
    nb = L // ATT_BAND
    W = W_BRANCH
    ncol = C // W
    ng = len(DIL_GROUPS)
    x = qkv.reshape(B, L, dil * C)

    def col(which):
        return lambda b, r, n: (b, n, r * ncol + which * ng + g)

    def col_prev(which):
        return lambda b, r, n: (b, jnp.maximum(n - 1, 0), r * ncol + which * ng + g)

    blk = (None, ATT_BAND, W)
    o, lse = pl.pallas_call(
        _attn_kernel,
        grid=(B, dil, nb),
        in_specs=[
            pl.BlockSpec(blk, col(0)),
            pl.BlockSpec(blk, col_prev(1)),
            pl.BlockSpec(blk, col(1)),
            pl.BlockSpec(blk, col_prev(2)),
            pl.BlockSpec(blk, col(2)),
            _const_spec(bias.shape),
        ],
        out_specs=[
            pl.BlockSpec(blk, lambda b, r, n: (b, n, r)),
            pl.BlockSpec((None, ATT_BAND, V7X_LANES), lambda b, r, n: (b, n, r)),
        ],
        out_shape=[
            jax.ShapeDtypeStruct((B, L, dil * W), BF16),
            jax.ShapeDtypeStruct((B, L, dil * V7X_LANES), F32),
        ],
        compiler_params=_params("parallel", "parallel", "parallel"),
        name=f"attn_d{dil}",
    )(x, x, x, x, x, bias)
    return o.reshape(B, S, W), lse.reshape(B, S, V7X_LANES)


def _attn_mix_kernel(o1_ref, o2_ref, o3_ref, l1_ref, l2_ref, l3_ref, out_ref):
    lane = lax.broadcasted_iota(jnp.int32, (1, V7X_LANES), 1)
    low = lane < ATT_HEAD_DIM
    o_refs = (o1_ref, o2_ref, o3_ref)
    lses = (l1_ref[...], l2_ref[...], l3_ref[...])
    for j in range(ATT_HEADS // 2):
        cols = slice(j * V7X_LANES, (j + 1) * V7X_LANES)
        wts = []
        for e in range(2):
            h = 2 * j + e
            ls = [x[:, h:h + 1] for x in lses]
            m = jnp.maximum(jnp.maximum(ls[0], ls[1]), ls[2])
            ex = [jnp.exp(x - m) for x in ls]
            inv = 1.0 / (ex[0] + ex[1] + ex[2])
            wts.append([x * inv for x in ex])
        acc = None
        for gi in range(3):
            w = jnp.where(low, wts[0][gi], wts[1][gi])
            term = w * o_refs[gi][:, cols].astype(F32)
            acc = term if acc is None else acc + term
        out_ref[:, cols] = acc.astype(BF16)


def _attn_mix(os_, lses):
    B, S, W = os_[0].shape
    tm = TOKEN_BLOCK
    ob = pl.BlockSpec((None, tm, W), lambda b, i: (b, i, 0))
    lb = pl.BlockSpec((None, tm, V7X_LANES), lambda b, i: (b, i, 0))
    return pl.pallas_call(
        _attn_mix_kernel,
        grid=(B, S // tm),
        in_specs=[ob, ob, ob, lb, lb, lb],
        out_specs=ob,
        out_shape=jax.ShapeDtypeStruct((B, S, W), BF16),
        compiler_params=_params("parallel", "parallel"),
        name="attn_mix",
    )(*os_, *lses)


def _s5_kernel(u_ref, wbr_ref, wbi_ref, ar_ref, ai_ref, cr_ref, ci_ref, dsk_ref, wglu_ref, bglu_ref,
               out_ref, h_ref, bu_ref, g_ref):
    nst = SSM_STATES
    rows = u_ref.shape[0]
    nbatch = h_ref.shape[0]
    steps = rows // nbatch

    @pl.when(pl.program_id(0) == 0)
    def _():
        h_ref[...] = jnp.zeros_like(h_ref)

    tile = V7X_MXU_DIM
    states_per_chan_tile = tile * SSM_STATE // SSM_GROUP
    for j in range(nst // tile):
        kt = (j * tile // states_per_chan_tile) * tile
        uk = u_ref[:, kt:kt + tile]
        bu_ref[:, j * tile:(j + 1) * tile] = _dot(uk, wbr_ref[kt:kt + tile, j * tile:(j + 1) * tile])
        bu_ref[:, nst + j * tile:nst + (j + 1) * tile] = _dot(uk, wbi_ref[kt:kt + tile, j * tile:(j + 1) * tile])

    cw = S5_LANE_CHUNK
    for c in range(nst // cw):
        lo = c * cw
        ar = jnp.broadcast_to(ar_ref[:, lo:lo + cw], (nbatch, cw))
        ai = jnp.broadcast_to(ai_ref[:, lo:lo + cw], (nbatch, cw))

        def body(t, carry, lo=lo, ar=ar, ai=ai):
            hr, hi = carry
            r0 = pl.multiple_of(t * nbatch, nbatch)
            br = bu_ref[pl.ds(r0, nbatch), lo:lo + cw]
            bi = bu_ref[pl.ds(r0, nbatch), nst + lo:nst + lo + cw]
            nr = ar * hr - ai * hi + br
            ni = ar * hi + ai * hr + bi
            bu_ref[pl.ds(r0, nbatch), lo:lo + cw] = nr
            bu_ref[pl.ds(r0, nbatch), nst + lo:nst + lo + cw] = ni
            return nr, ni

        hr, hi = lax.fori_loop(0, steps, body, (h_ref[:, lo:lo + cw], h_ref[:, nst + lo:nst + lo + cw]), unroll=4)
        h_ref[:, lo:lo + cw] = hr
        h_ref[:, nst + lo:nst + lo + cw] = hi

    chan_tile_states = tile * SSM_STATE // SSM_GROUP
    for n in range(W_BRANCH // tile):
        s0 = n * chan_tile_states
        cols = slice(n * tile, (n + 1) * tile)
        yr = _dot(bu_ref[:, s0:s0 + chan_tile_states].astype(BF16), cr_ref[s0:s0 + chan_tile_states, cols])
        yi = _dot(bu_ref[:, nst + s0:nst + s0 + chan_tile_states].astype(BF16), ci_ref[s0:s0 + chan_tile_states, cols])
        y = yr - yi + u_ref[:, cols].astype(F32) * dsk_ref[:, cols]
        g_ref[:, cols] = _gelu(y)
    g = g_ref[...]
    z = _dot(g.astype(BF16), wglu_ref[...]) + bglu_ref[...]
    out_ref[...] = (g * _sigmoid(z)).astype(BF16)


def _s5(u_tb, nbatch, wbr, wbi, ar, ai, cr, ci, dskip, wglu, bglu):
    rows_total, W = u_tb.shape
    rows = S5_STEPS * nbatch
    return pl.pallas_call(
        _s5_kernel,
        grid=(rows_total // rows,),
        in_specs=[pl.BlockSpec((rows, W), lambda k: (k, 0))]
        + [_const_spec(a.shape) for a in (wbr, wbi, ar, ai, cr, ci, dskip, wglu, bglu)],
        out_specs=pl.BlockSpec((rows, W), lambda k: (k, 0)),
        out_shape=jax.ShapeDtypeStruct((rows_total, W), BF16),
        scratch_shapes=[
            pltpu.VMEM((nbatch, 2 * SSM_STATES), F32),
            pltpu.VMEM((rows, 2 * SSM_STATES), F32),
            pltpu.VMEM((rows, W), F32),
        ],
        compiler_params=_params("arbitrary"),
        name="s5",
    )(u_tb, wbr, wbi, ar, ai, cr, ci, dskip, wglu, bglu)


def _s5_params(a_re, a_im, log_dt, b_re, b_im, c_re, c_im):
    G, P, C = SSM_GROUPS, SSM_STATE, SSM_GROUP
    lam_re = jnp.minimum(a_re, -1e-4)
    lam_im = a_im
    dt = jnp.exp(log_dt)[:, None]
    mag = jnp.exp(lam_re * dt)
    ab_re, ab_im = mag * jnp.cos(lam_im * dt), mag * jnp.sin(lam_im * dt)
    den = lam_re * lam_re + lam_im * lam_im
    f_re = ((ab_re - 1.0) * lam_re + ab_im * lam_im) / den
    f_im = (ab_im * lam_re - (ab_re - 1.0) * lam_im) / den
    bb_re = f_re[..., None] * b_re - f_im[..., None] * b_im
    bb_im = f_re[..., None] * b_im + f_im[..., None] * b_re
    eye = jnp.eye(G, dtype=F32)

    def in_mat(bb):
        return jnp.einsum('gpc,gh->gchp', bb, eye).reshape(G * C, G * P).astype(BF16)

    def out_mat(cc):
        return jnp.einsum('gcp,gh->gphc', cc, eye).reshape(G * P, G * C).astype(BF16)

    return (in_mat(bb_re), in_mat(bb_im), ab_re.reshape(1, G * P), ab_im.reshape(1, G * P),
            out_mat(c_re), out_mat(c_im))


def _sgu_kernel(z_ref, lng_ref, lnb_ref, ws_ref, bst_ref, out_ref):
    tb = z_ref.shape[0]
    T = SGU_CHUNK
    gd = W_BRANCH // SGU_GROUPS
    z = _gelu(z_ref[...].astype(F32))
    u = z[:, :W_BRANCH]
    v = z[:, W_BRANCH:]
    mu = jnp.mean(v, axis=-1, keepdims=True)
    vc = v - mu
    var = jnp.mean(vc * vc, axis=-1, keepdims=True)
    vn = (vc * lax.rsqrt(var + EPS) * lng_ref[...] + lnb_ref[...]).astype(BF16)
    causal = lax.broadcasted_iota(jnp.int32, (T, T), 0) >= lax.broadcasted_iota(jnp.int32, (T, T), 1)
    for g in range(SGU_GROUPS):
        wg = jnp.where(causal, ws_ref[g], 0.0).astype(BF16)
        bias = bst_ref[:, g:g + 1]
        for c in range(tb // T):
            rows = slice(c * T, (c + 1) * T)
            cols = slice(g * gd, (g + 1) * gd)
            sv = _dot(wg, vn[rows, cols]) + bias
            out_ref[rows, cols] = (u[rows, cols] * sv).astype(BF16)


def _sgu(z, ln_g, ln_b, w_s, b_s_t):
    B, S, W2 = z.shape
    tb = TOKEN_BLOCK
    return pl.pallas_call(
        _sgu_kernel,
        grid=(B, S // tb),
        in_specs=[
            pl.BlockSpec((None, tb, W2), lambda b, i: (b, i, 0)),
            _const_spec((1, W_BRANCH)),
            _const_spec((1, W_BRANCH)),
            _const_spec(w_s.shape),
            _const_spec(b_s_t.shape),
        ],
        out_specs=pl.BlockSpec((None, tb, W_BRANCH), lambda b, i: (b, i, 0)),
        out_shape=jax.ShapeDtypeStruct((B, S, W_BRANCH), BF16),
        compiler_params=_params("parallel", "parallel"),
        name="sgu",
    )(z, ln_g, ln_b, w_s, b_s_t)


def _merge_kernel(x_ref, a_ref, b_ref, c_ref, d_ref, gpre_ref, wg_ref, gb_ref, wup_ref, wout_ref, gpost_ref,
                  out_ref):
    x = x_ref[...]
    h = _rms(x, gpre_ref[...]).astype(BF16)
    merged = None
    for i, br in enumerate((a_ref, b_ref, c_ref, d_ref)):
        gate = _sigmoid(_dot(h, wg_ref[:, i * D_MODEL:(i + 1) * D_MODEL]) + gb_ref[i:i + 1, :])
        term = gate * _dot(br[...], wup_ref[i])
        merged = term if merged is None else merged + term
    y = _dot(merged.astype(BF16), wout_ref[...])
    out_ref[...] = x + _rms(y, gpost_ref[...])


def _merge(x, a, b, c_tb, d, gpre, wg, gb, wup, wout, gpost):
    B, S, D = x.shape
    tm = TOKEN_BLOCK
    xb = pl.BlockSpec((None, tm, D), lambda b_, i: (b_, i, 0))
    br = pl.BlockSpec((None, tm, W_BRANCH), lambda b_, i: (b_, i, 0))
    return pl.pallas_call(
        _merge_kernel,
        grid=(B, S // tm),
        in_specs=[
            xb, br, br,
            pl.BlockSpec((tm, W_BRANCH), lambda b_, i: (i, b_)),
            br,
            _const_spec((1, D)), _const_spec(wg.shape), _const_spec(gb.shape), _const_spec(wup.shape),
            _const_spec(wout.shape), _const_spec((1, D)),
        ],
        out_specs=xb,
        out_shape=jax.ShapeDtypeStruct((B, S, D), F32),
        compiler_params=_params("parallel", "parallel"),
        name="merge",
    )(x, a, b, c_tb, d, gpre, wg, gb, wup, wout, gpost)


def _mem_kv_kernel(mem_ref, g_ref, w_ref, k_ref, v_ref):
    mn = _rms(mem_ref[...], g_ref[...]).astype(BF16)
    hw = X_HEADS * X_HEAD_DIM
    k_ref[...] = _dot(mn, w_ref[:, :hw]).astype(BF16)
    v_ref[...] = _dot(mn, w_ref[:, hw:]).astype(BF16)


def _mem_kv(mem, g, w):
    B, M, D = mem.shape
    hw = X_HEADS * X_HEAD_DIM
    ob = pl.BlockSpec((None, M, hw), lambda b: (b, 0, 0))
    return pl.pallas_call(
        _mem_kv_kernel,
        grid=(B,),
        in_specs=[pl.BlockSpec((None, M, D), lambda b: (b, 0, 0)), _const_spec((1, D)), _const_spec(w.shape)],
        out_specs=[ob, ob],
        out_shape=[jax.ShapeDtypeStruct((B, M, hw), BF16)] * 2,
        compiler_params=_params("parallel"),
        name="mem_kv",
    )(mem, g, w)


def _xattn_kernel(x_ref, k_ref, v_ref, gpre_ref, wq_ref, wo_ref, gpost_ref, out_ref, o_ref):
    x = x_ref[...]
    h = _rms(x, gpre_ref[...]).astype(BF16)
    q = (_dot(h, wq_ref[...]) * (X_HEAD_DIM ** -0.5)).astype(BF16)
    for hd in range(X_HEADS):
        cols = slice(hd * X_HEAD_DIM, (hd + 1) * X_HEAD_DIM)
        s = lax.dot_general(q[:, cols], k_ref[:, cols], (((1,), (1,)), ((), ())), preferred_element_type=F32)
        m = jnp.max(s, axis=-1, keepdims=True)
        p = jnp.exp(s - m)
        l = jnp.sum(p, axis=-1, keepdims=True)
        o_ref[:, cols] = (_dot(p.astype(BF16), v_ref[:, cols]) * (1.0 / l)).astype(BF16)
    y = _dot(o_ref[...], wo_ref[...])
    out_ref[...] = x + _rms(y, gpost_ref[...])


def _xattn(x, k, v, gpre, wq, wo, gpost):
    B, S, D = x.shape
    tm = TOKEN_BLOCK
    hw = X_HEADS * X_HEAD_DIM
    xb = pl.BlockSpec((None, tm, D), lambda b, i: (b, i, 0))
    kb = pl.BlockSpec((None, N_MEM, hw), lambda b, i: (b, 0, 0))
    return pl.pallas_call(
        _xattn_kernel,
        grid=(B, S // tm),
        in_specs=[xb, kb, kb, _const_spec((1, D)), _const_spec(wq.shape), _const_spec(wo.shape),
                  _const_spec((1, D))],
        out_specs=xb,
        out_shape=jax.ShapeDtypeStruct((B, S, D), F32),
        scratch_shapes=[pltpu.VMEM((tm, hw), BF16)],
        compiler_params=_params("parallel", "parallel"),
        name="xattn",
    )(x, k, v, gpre, wq, wo, gpost)


def _ffn_kernel(x_ref, gpre_ref, w1_ref, w2_ref, gpost_ref, out_ref, acc_ref):
    x = x_ref[...]
    h = _rms(x, gpre_ref[...]).astype(BF16)
    for c in range(D_FF // FF_CHUNK):
        cols = slice(c * FF_CHUNK, (c + 1) * FF_CHUNK)
        a = jnp.maximum(_dot(h, w1_ref[:, cols]), 0.0)
        a = (a * a).astype(BF16)
        part = _dot(a, w2_ref[cols, :])
        if c == 0:
            acc_ref[...] = part
        else:
            acc_ref[...] += part
    out_ref[...] = x + _rms(acc_ref[...], gpost_ref[...])


def _ffn(x, gpre, w1, w2, gpost):
    B, S, D = x.shape
    tm = TOKEN_BLOCK
    xb = pl.BlockSpec((None, tm, D), lambda b, i: (b, i, 0))
    return pl.pallas_call(
        _ffn_kernel,
        grid=(B, S // tm),
        in_specs=[xb, _const_spec((1, D)), _const_spec(w1.shape), _const_spec(w2.shape), _const_spec((1, D))],
        out_specs=xb,
        out_shape=jax.ShapeDtypeStruct((B, S, D), F32),
        scratch_shapes=[pltpu.VMEM((tm, D), F32)],
        compiler_params=_params("parallel", "parallel"),
        name="ffn",
    )(x, gpre, w1, w2, gpost)


def _t5_bucket(n):
    exact = REL_BUCKETS // 2
    nf = np.maximum(n, 1).astype(np.float32)
    large = exact + (np.log(nf / exact) / np.log(REL_MAX_DIST / exact) * (REL_BUCKETS - exact)).astype(np.int32)
    large = np.minimum(large, REL_BUCKETS - 1)
    return np.where(n < exact, n, large).astype(np.int32)


def _band_bias(rel_bias, g, band, dil):
    i = np.arange(band)[:, None]
    kk = np.arange(2 * band)[None, :]
    dist = band + i - kk
    local = (dist >= 0) & (dist <= band)
    bucket = _t5_bucket(np.clip(dist, 0, band) * dil)
    bias = rel_bias[jnp.asarray(bucket)][..., g * ATT_HEADS:(g + 1) * ATT_HEADS]
    bias = bias.transpose(2, 0, 1).astype(F32)
    return jnp.where(jnp.asarray(local)[None], bias, NEG_INF)


def kernel(x, mem, rel_bias, g_mix_pre, g_mix_post, w_in, gate_b, pool_w, pool_scale, a_re, a_im, log_dt,
           b_re, b_im, c_re, c_im, d_skip, w_glu, b_glu, sgu_ln_g, sgu_ln_b, w_s, b_s, w_up, w_out,
           g_x_pre, g_x_post, g_mem, w_cq, w_ckv, w_co, g_ff_pre, g_ff_post, w_ff1, w_ff2):
    B, S, D = x.shape
    depth = w_in.shape[0]
    assert D == D_MODEL and S % (TOKEN_BLOCK) == 0 and B == V7X_SUBLANES
    for win, dil in DIL_GROUPS:
        assert win // dil == ATT_BAND and (S // dil) % ATT_BAND == 0

    biases = [_band_bias(rel_bias, g, win // dil, dil) for g, (win, dil) in enumerate(DIL_GROUPS)]
    row = lambda a: a.reshape(1, -1).astype(F32)

    for l in range(depth):
        w_in_l = w_in[l].astype(BF16)
        pool_in, qkv, ssm_tb, sgu_in = _in_proj(x, row(g_mix_pre[l]), w_in_l[:, :OFF_GATE])
        a_out = _pool(pool_in, pool_w[l].astype(BF16), row(pool_scale[l]))
        outs, lses = [], []
        for g, (win, dil) in enumerate(DIL_GROUPS):
            o, lse = _attn_group(qkv, biases[g], g, dil)
            outs.append(o)
            lses.append(lse)
        b_out = _attn_mix(outs, lses)
        s5p = _s5_params(a_re[l], a_im[l], log_dt[l], b_re[l], b_im[l], c_re[l], c_im[l])
        c_tb = _s5(ssm_tb.reshape(S * B, W_BRANCH), B, *s5p, row(d_skip[l]), w_glu[l].astype(BF16), row(b_glu[l]))
        d_out = _sgu(sgu_in, row(sgu_ln_g[l]), row(sgu_ln_b[l]), w_s[l], b_s[l].T)
        x = _merge(x, a_out, b_out, c_tb.reshape(S, B * W_BRANCH), d_out, row(g_mix_pre[l]),
                   w_in_l[:, OFF_GATE:], gate_b[l], w_up[l].astype(BF16), w_out[l].astype(BF16),
                   row(g_mix_post[l]))
        k_mem, v_mem = _mem_kv(mem, row(g_mem[l]), w_ckv[l].astype(BF16))
        x = _xattn(x, k_mem, v_mem, row(g_x_pre[l]), w_cq[l].astype(BF16), w_co[l].astype(BF16),
                   row(g_x_post[l]))
        x = _ffn(x, row(g_ff_pre[l]), w_ff1[l].astype(BF16), w_ff2[l].astype(BF16), row(g_ff_post[l]))
    return x
```

```python
import functools
import math

import jax
import jax.numpy as jnp
import numpy as np
from jax import lax
from jax.experimental import pallas as pl
from jax.experimental.pallas import tpu as pltpu

F32 = jnp.float32
BF16 = jnp.bfloat16

D_MODEL = 1024
N_MEM = 256
N_BRANCH = 4
W_BRANCH = D_MODEL // 2
POOL_WINDOWS = (2, 4, 8, 16)
POOL_GROUP = W_BRANCH // len(POOL_WINDOWS)
DIL_GROUPS = ((128, 1), (512, 4), (2048, 16))
ATT_HEADS = 8
ATT_HEAD_DIM = W_BRANCH // ATT_HEADS
SSM_GROUP = 16
SSM_GROUPS = W_BRANCH // SSM_GROUP
SSM_STATE = 64
SSM_STATES = SSM_GROUPS * SSM_STATE
SGU_CHUNK = 128
SGU_GROUPS = 4
X_HEADS = 4
X_HEAD_DIM = 128
D_FF = 4 * D_MODEL
REL_BUCKETS = 32
REL_MAX_DIST = 2048
EPS = 1e-6
NEG_INF = -1e30
N_ATT_COLS = 3 * len(DIL_GROUPS) * W_BRANCH
OFF_GATE = W_BRANCH + N_ATT_COLS + W_BRANCH + 2 * W_BRANCH

V7X_LANES = 128
V7X_SUBLANES = 8
V7X_MXU_DIM = 256
V7X_VMEM_LIMIT_BYTES = 56 * 1024 * 1024

TOKEN_BLOCK = 512
POOL_HALO = 16
ATT_BAND = 128
ATT_SPAN = 512
S5_STEPS = 64
S5_LANE_CHUNK = 512
FF_CHUNK = 512


def _params(*sem):
    return pltpu.CompilerParams(dimension_semantics=sem, vmem_limit_bytes=V7X_VMEM_LIMIT_BYTES)


def _const_spec(shape):
    nd = len(shape)
    return pl.BlockSpec(shape, lambda *_: (0,) * nd, pipeline_mode=pl.Buffered(1))


def _rms(xf, g):
    return xf * lax.rsqrt(jnp.mean(xf * xf, axis=-1, keepdims=True) + EPS) * g


def _gelu(x):
    c = math.sqrt(2.0 / math.pi)
    return 0.5 * x * (1.0 + jnp.tanh(c * (x + 0.044715 * (x * x * x))))


def _sigmoid(x):
    return 1.0 / (1.0 + jnp.exp(-x))


def _dot(a, b):
    return jnp.dot(a, b, preferred_element_type=F32)


def _in_proj_kernel(x_ref, g_ref, w_ref, pool_ref, qkv1_ref, qkv2_ref, qkv3_ref, ssm_ref, sgu_ref, slab_ref):
    h = _rms(x_ref[...], g_ref[...]).astype(BF16)
    tm = x_ref.shape[0]
    step = W_BRANCH
    ng = len(DIL_GROUPS)
    nslab = step // V7X_LANES
    pool_ref[...] = _dot(h, w_ref[:, 0:step]).astype(BF16)
    for which in range(3):
        dst = slice(which * step, (which + 1) * step)
        for g, out_ref in enumerate((qkv1_ref, qkv2_ref, qkv3_ref)):
            lo = W_BRANCH + (which * ng + g) * step
            y = _dot(h, w_ref[:, lo:lo + step])
            dil = DIL_GROUPS[g][1]
            if dil == 1:
                out_ref[:, dst] = y.astype(BF16)
                continue
            for c in range(nslab):
                slab_ref[c] = y[:, c * V7X_LANES:(c + 1) * V7X_LANES]
            for c in range(nslab):
                cols = slice(which * step + c * V7X_LANES, which * step + (c + 1) * V7X_LANES)
                for r in range(dil):
                    out_ref[r, :, cols] = slab_ref[c, pl.ds(r, tm // dil, stride=dil), :].astype(BF16)
    lo = W_BRANCH + N_ATT_COLS
    ssm_ref[...] = _dot(h, w_ref[:, lo:lo + step]).astype(BF16)
    lo += W_BRANCH
    for j in range(2):
        sgu_ref[:, j * step:(j + 1) * step] = _dot(h, w_ref[:, lo + j * step:lo + (j + 1) * step]).astype(BF16)


def _residue_major_shape(B, S, dil, width):
    return (B, S // ATT_SPAN, dil, ATT_SPAN // dil, width)


def _in_proj(x, g, w):
    B, S, D = x.shape
    tm = ATT_SPAN
    qw = 3 * W_BRANCH
    d2, d3 = DIL_GROUPS[1][1], DIL_GROUPS[2][1]
    return pl.pallas_call(
        _in_proj_kernel,
        grid=(B, S // tm),
        in_specs=[
            pl.BlockSpec((None, tm, D), lambda b, i: (b, i, 0)),
            _const_spec((1, D)),
            _const_spec((D, OFF_GATE)),
        ],
        out_specs=[
            pl.BlockSpec((None, tm, W_BRANCH), lambda b, i: (b, i, 0)),
            pl.BlockSpec((None, tm, qw), lambda b, i: (b, i, 0)),
            pl.BlockSpec((None, None, d2, tm // d2, qw), lambda b, i: (b, i, 0, 0, 0)),
            pl.BlockSpec((None, None, d3, tm // d3, qw), lambda b, i: (b, i, 0, 0, 0)),
            pl.BlockSpec((tm, W_BRANCH), lambda b, i: (i, b)),
            pl.BlockSpec((None, tm, 2 * W_BRANCH), lambda b, i: (b, i, 0)),
        ],
        out_shape=[
            jax.ShapeDtypeStruct((B, S, W_BRANCH), BF16),
            jax.ShapeDtypeStruct((B, S, qw), BF16),
            jax.ShapeDtypeStruct(_residue_major_shape(B, S, d2, qw), BF16),
            jax.ShapeDtypeStruct(_residue_major_shape(B, S, d3, qw), BF16),
            jax.ShapeDtypeStruct((S, B * W_BRANCH), BF16),
            jax.ShapeDtypeStruct((B, S, 2 * W_BRANCH), BF16),
        ],
        scratch_shapes=[pltpu.VMEM((W_BRANCH // V7X_LANES, tm, V7X_LANES), F32)],
        compiler_params=_params("parallel", "parallel"),
        name="in_proj",
    )(x, g, w)


def _pool_kernel(cur_ref, halo_ref, w_ref, scale_ref, out_ref, ext_ref):
    i = pl.program_id(1)
    tb = cur_ref.shape[0]
    halo = halo_ref[...].astype(F32)
    ext_ref[0:POOL_HALO, :] = jnp.where(i == 0, 0.0, halo)
    ext_ref[POOL_HALO:, :] = cur_ref[...].astype(F32)
    tpos = i * tb + lax.broadcasted_iota(jnp.int32, (tb, 1), 0)
    for gi, win in enumerate(POOL_WINDOWS):
        lanes = slice(gi * POOL_GROUP, (gi + 1) * POOL_GROUP)
        tok = ext_ref[POOL_HALO:POOL_HALO + tb, lanes]
        s = tok
        for j in range(1, win):
            s = s + ext_ref[POOL_HALO - j:POOL_HALO - j + tb, lanes]
        cnt = jnp.minimum(tpos + 1, win).astype(F32)
        p = s / cnt - tok
        y = _dot(p.astype(BF16), w_ref[gi])
        out_ref[:, lanes] = (y * scale_ref[:, lanes]).astype(BF16)


def _pool(pool_in, w_pool, scale):
    B, S, W = pool_in.shape
    tb = TOKEN_BLOCK
    hb = tb // POOL_HALO
    return pl.pallas_call(
        _pool_kernel,
        grid=(B, S // tb),
        in_specs=[
            pl.BlockSpec((None, tb, W), lambda b, i: (b, i, 0)),
            pl.BlockSpec((None, POOL_HALO, W), lambda b, i: (b, jnp.maximum(i * hb - 1, 0), 0)),
            _const_spec(w_pool.shape),
            _const_spec((1, W)),
        ],
        out_specs=pl.BlockSpec((None, tb, W), lambda b, i: (b, i, 0)),
        out_shape=jax.ShapeDtypeStruct((B, S, W), BF16),
        scratch_shapes=[pltpu.VMEM((POOL_HALO + tb, W), F32)],
        compiler_params=_params("parallel", "parallel"),
        name="pool",
    )(pool_in, pool_in, w_pool, scale)


def _attn_kernel(q_ref, kp_ref, kc_ref, vp_ref, vc_ref, bias_ref, o_ref, lse_ref):
    n = pl.program_id(2)
    band = ATT_BAND
    rows = lambda ref: ref[...].reshape(band, W_BRANCH)
    q = rows(q_ref)
    k = jnp.concatenate([rows(kp_ref), rows(kc_ref)], axis=0)
    v = jnp.concatenate([rows(vp_ref), rows(vc_ref)], axis=0)
    lane = lax.broadcasted_iota(jnp.int32, (1, V7X_LANES), 1)
    low = lane < ATT_HEAD_DIM
    no_prev = jnp.logical_and(n == 0, lax.broadcasted_iota(jnp.int32, (1, 2 * band), 1) < band)
    scale = ATT_HEAD_DIM ** -0.5
    lse_tile = jnp.zeros((band, V7X_LANES), F32)
    for j in range(ATT_HEADS // 2):
        cols = slice(j * V7X_LANES, (j + 1) * V7X_LANES)
        qp, kp, vp = q[:, cols], k[:, cols], v[:, cols]
        o_pair = jnp.zeros((band, V7X_LANES), F32)
        for e in range(2):
            sel = low if e == 0 else jnp.logical_not(low)
            qh = jnp.where(sel, qp, jnp.zeros_like(qp))
            s = lax.dot_general(qh, kp, (((1,), (1,)), ((), ())), preferred_element_type=F32)
            s = s * scale + bias_ref[2 * j + e]
            s = jnp.where(no_prev, NEG_INF, s)
            m = jnp.max(s, axis=-1, keepdims=True)
            p = jnp.exp(s - m)
            l = jnp.sum(p, axis=-1, keepdims=True)
            vh = jnp.where(sel, vp, jnp.zeros_like(vp))
            o_pair = o_pair + _dot(p.astype(BF16), vh) * (1.0 / l)
            lse_tile = jnp.where(lane == 2 * j + e, m + jnp.log(l), lse_tile)
        o_ref[..., cols] = o_pair.astype(BF16).reshape(o_ref.shape[:-1] + (V7X_LANES,))
    lse_ref[...] = lse_tile.reshape(lse_ref.shape)


def _attn_group(qkv, bias, dil):
    B = qkv.shape[0]
    W = W_BRANCH
    if dil == 1:
        S = qkv.shape[1]
        nb = S // ATT_BAND
        blk = lambda width: (None, ATT_BAND, width)
        idx = lambda which: (lambda b, r, n: (b, n, which))
        idx_prev = lambda which: (lambda b, r, n: (b, jnp.maximum(n - 1, 0), which))
        out_shape = lambda width: (B, S, width)
    else:
        nspan, _, per, _ = qkv.shape[1:]
        spans = ATT_BAND // per
        nb = nspan // spans
        blk = lambda width: (None, spans, None, per, width)
        idx = lambda which: (lambda b, r, n: (b, n, r, 0, which))
        idx_prev = lambda which: (lambda b, r, n: (b, jnp.maximum(n - 1, 0), r, 0, which))
        out_shape = lambda width: (B, nspan, dil, per, width)
    return pl.pallas_call(
        _attn_kernel,
        grid=(B, dil, nb),
        in_specs=[
            pl.BlockSpec(blk(W), idx(0)),
            pl.BlockSpec(blk(W), idx_prev(1)),
            pl.BlockSpec(blk(W), idx(1)),
            pl.BlockSpec(blk(W), idx_prev(2)),
            pl.BlockSpec(blk(W), idx(2)),
            _const_spec(bias.shape),
        ],
        out_specs=[
            pl.BlockSpec(blk(W), idx(0)),
            pl.BlockSpec(blk(V7X_LANES), idx(0)),
        ],
        out_shape=[
            jax.ShapeDtypeStruct(out_shape(W), BF16),
            jax.ShapeDtypeStruct(out_shape(V7X_LANES), F32),
        ],
        compiler_params=_params("parallel", "parallel", "parallel"),
        name=f"attn_d{dil}",
    )(qkv, qkv, qkv, qkv, qkv, bias)


def _attn_mix_kernel(o1_ref, o2_ref, o3_ref, l1_ref, l2_ref, l3_ref, out_ref, on_ref, ln_ref):
    lane = lax.broadcasted_iota(jnp.int32, (1, V7X_LANES), 1)
    low = lane < ATT_HEAD_DIM
    nslab = W_BRANCH // V7X_LANES
    for gi, (o_ref, l_ref) in enumerate(((o2_ref, l2_ref), (o3_ref, l3_ref))):
        dil, per = o_ref.shape[0], o_ref.shape[1]
        for r in range(dil):
            ln_ref[gi, pl.ds(r, per, stride=dil), :] = l_ref[r]
            for c in range(nslab):
                on_ref[gi * nslab + c, pl.ds(r, per, stride=dil), :] = (
                    o_ref[r, :, c * V7X_LANES:(c + 1) * V7X_LANES].astype(F32))
    lses = (l1_ref[...], ln_ref[0], ln_ref[1])
    for j in range(ATT_HEADS // 2):
        cols = slice(j * V7X_LANES, (j + 1) * V7X_LANES)
        wts = []
        for e in range(2):
            h = 2 * j + e
            ls = [x[:, h:h + 1] for x in lses]
            m = jnp.maximum(jnp.maximum(ls[0], ls[1]), ls[2])
            ex = [jnp.exp(x - m) for x in ls]
            inv = 1.0 / (ex[0] + ex[1] + ex[2])
            wts.append([x * inv for x in ex])
        vals = (o1_ref[:, cols].astype(F32), on_ref[j], on_ref[nslab + j])
        acc = None
        for gi in range(3):
            term = jnp.where(low, wts[0][gi], wts[1][gi]) * vals[gi]
            acc = term if acc is None else acc + term
        out_ref[:, cols] = acc.astype(BF16)


def _attn_mix(os_, lses):
    B, S, W = os_[0].shape
    tm = ATT_SPAN

    def spec(a):
        if a.ndim == 3:
            return pl.BlockSpec((None, tm, a.shape[-1]), lambda b, i: (b, i, 0))
        return pl.BlockSpec((None, None) + a.shape[2:], lambda b, i: (b, i, 0, 0, 0))

    return pl.pallas_call(
        _attn_mix_kernel,
        grid=(B, S // tm),
        in_specs=[spec(a) for a in (*os_, *lses)],
        out_specs=pl.BlockSpec((None, tm, W), lambda b, i: (b, i, 0)),
        out_shape=jax.ShapeDtypeStruct((B, S, W), BF16),
        scratch_shapes=[
            pltpu.VMEM((2 * W // V7X_LANES, tm, V7X_LANES), F32),
            pltpu.VMEM((2, tm, V7X_LANES), F32),
        ],
        compiler_params=_params("parallel", "parallel"),
        name="attn_mix",
    )(*os_, *lses)


def _s5_kernel(u_ref, wbr_ref, wbi_ref, ar_ref, ai_ref, cr_ref, ci_ref, dsk_ref, wglu_ref, bglu_ref,
               out_ref, h_ref, bu_ref, g_ref):
    nst = SSM_STATES
    rows = u_ref.shape[0]
    nbatch = h_ref.shape[0]
    steps = rows // nbatch

    @pl.when(pl.program_id(0) == 0)
    def _():
        h_ref[...] = jnp.zeros_like(h_ref)

    tile = V7X_MXU_DIM
    states_per_chan_tile = tile * SSM_STATE // SSM_GROUP
    for j in range(nst // tile):
        kt = (j * tile // states_per_chan_tile) * tile
        uk = u_ref[:, kt:kt + tile]
        bu_ref[:, j * tile:(j + 1) * tile] = _dot(uk, wbr_ref[kt:kt + tile, j * tile:(j + 1) * tile])
        bu_ref[:, nst + j * tile:nst + (j + 1) * tile] = _dot(uk, wbi_ref[kt:kt + tile, j * tile:(j + 1) * tile])

    cw = S5_LANE_CHUNK
    for c in range(nst // cw):
        lo = c * cw
        ar = jnp.broadcast_to(ar_ref[:, lo:lo + cw], (nbatch, cw))
        ai = jnp.broadcast_to(ai_ref[:, lo:lo + cw], (nbatch, cw))

        def body(t, carry, lo=lo, ar=ar, ai=ai):
            hr, hi = carry
            r0 = pl.multiple_of(t * nbatch, nbatch)
            br = bu_ref[pl.ds(r0, nbatch), lo:lo + cw]
            bi = bu_ref[pl.ds(r0, nbatch), nst + lo:nst + lo + cw]
            nr = ar * hr - ai * hi + br
            ni = ar * hi + ai * hr + bi
            bu_ref[pl.ds(r0, nbatch), lo:lo + cw] = nr
            bu_ref[pl.ds(r0, nbatch), nst + lo:nst + lo + cw] = ni
            return nr, ni

        hr, hi = lax.fori_loop(0, steps, body, (h_ref[:, lo:lo + cw], h_ref[:, nst + lo:nst + lo + cw]), unroll=4)
        h_ref[:, lo:lo + cw] = hr
        h_ref[:, nst + lo:nst + lo + cw] = hi

    chan_tile_states = tile * SSM_STATE // SSM_GROUP
    for n in range(W_BRANCH // tile):
        s0 = n * chan_tile_states
        cols = slice(n * tile, (n + 1) * tile)
        yr = _dot(bu_ref[:, s0:s0 + chan_tile_states].astype(BF16), cr_ref[s0:s0 + chan_tile_states, cols])
        yi = _dot(bu_ref[:, nst + s0:nst + s0 + chan_tile_states].astype(BF16), ci_ref[s0:s0 + chan_tile_states, cols])
        y = yr - yi + u_ref[:, cols].astype(F32) * dsk_ref[:, cols]
        g_ref[:, cols] = _gelu(y)
    g = g_ref[...]
    z = _dot(g.astype(BF16), wglu_ref[...]) + bglu_ref[...]
    out_ref[...] = (g * _sigmoid(z)).astype(BF16)


def _s5(u_tb, nbatch, wbr, wbi, ar, ai, cr, ci, dskip, wglu, bglu):
    rows_total, W = u_tb.shape
    rows = S5_STEPS * nbatch
    return pl.pallas_call(
        _s5_kernel,
        grid=(rows_total // rows,),
        in_specs=[pl.BlockSpec((rows, W), lambda k: (k, 0))]
        + [_const_spec(a.shape) for a in (wbr, wbi, ar, ai, cr, ci, dskip, wglu, bglu)],
        out_specs=pl.BlockSpec((rows, W), lambda k: (k, 0)),
        out_shape=jax.ShapeDtypeStruct((rows_total, W), BF16),
        scratch_shapes=[
            pltpu.VMEM((nbatch, 2 * SSM_STATES), F32),
            pltpu.VMEM((rows, 2 * SSM_STATES), F32),
            pltpu.VMEM((rows, W), F32),
        ],
        compiler_params=_params("arbitrary"),
        name="s5",
    )(u_tb, wbr, wbi, ar, ai, cr, ci, dskip, wglu, bglu)


def _s5_params(a_re, a_im, log_dt, b_re, b_im, c_re, c_im):
    G, P, C = SSM_GROUPS, SSM_STATE, SSM_GROUP
    lam_re = jnp.minimum(a_re, -1e-4)
    lam_im = a_im
    dt = jnp.exp(log_dt)[:, None]
    mag = jnp.exp(lam_re * dt)
    ab_re, ab_im = mag * jnp.cos(lam_im * dt), mag * jnp.sin(lam_im * dt)
    den = lam_re * lam_re + lam_im * lam_im
    f_re = ((ab_re - 1.0) * lam_re + ab_im * lam_im) / den
    f_im = (ab_im * lam_re - (ab_re - 1.0) * lam_im) / den
    bb_re = f_re[..., None] * b_re - f_im[..., None] * b_im
    bb_im = f_re[..., None] * b_im + f_im[..., None] * b_re
    eye = jnp.eye(G, dtype=F32)

    def in_mat(bb):
        return jnp.einsum('gpc,gh->gchp', bb, eye).reshape(G * C, G * P).astype(BF16)

    def out_mat(cc):
        return jnp.einsum('gcp,gh->gphc', cc, eye).reshape(G * P, G * C).astype(BF16)

    return (in_mat(bb_re), in_mat(bb_im), ab_re.reshape(1, G * P), ab_im.reshape(1, G * P),
            out_mat(c_re), out_mat(c_im))


def _sgu_kernel(z_ref, lng_ref, lnb_ref, ws_ref, bst_ref, out_ref):
    tb = z_ref.shape[0]
    T = SGU_CHUNK
    gd = W_BRANCH // SGU_GROUPS
    z = _gelu(z_ref[...].astype(F32))
    u = z[:, :W_BRANCH]
    v = z[:, W_BRANCH:]
    mu = jnp.mean(v, axis=-1, keepdims=True)
    vc = v - mu
    var = jnp.mean(vc * vc, axis=-1, keepdims=True)
    vn = (vc * lax.rsqrt(var + EPS) * lng_ref[...] + lnb_ref[...]).astype(BF16)
    causal = lax.broadcasted_iota(jnp.int32, (T, T), 0) >= lax.broadcasted_iota(jnp.int32, (T, T), 1)
    for g in range(SGU_GROUPS):
        wg = jnp.where(causal, ws_ref[g], 0.0).astype(BF16)
        bias = bst_ref[:, g:g + 1]
        for c in range(tb // T):
            rows = slice(c * T, (c + 1) * T)
            cols = slice(g * gd, (g + 1) * gd)
            sv = _dot(wg, vn[rows, cols]) + bias
            out_ref[rows, cols] = (u[rows, cols] * sv).astype(BF16)


def _sgu(z, ln_g, ln_b, w_s, b_s_t):
    B, S, W2 = z.shape
    tb = TOKEN_BLOCK
    return pl.pallas_call(
        _sgu_kernel,
        grid=(B, S // tb),
        in_specs=[
            pl.BlockSpec((None, tb, W2), lambda b, i: (b, i, 0)),
            _const_spec((1, W_BRANCH)),
            _const_spec((1, W_BRANCH)),
            _const_spec(w_s.shape),
            _const_spec(b_s_t.shape),
        ],
        out_specs=pl.BlockSpec((None, tb, W_BRANCH), lambda b, i: (b, i, 0)),
        out_shape=jax.ShapeDtypeStruct((B, S, W_BRANCH), BF16),
        compiler_params=_params("parallel", "parallel"),
        name="sgu",
    )(z, ln_g, ln_b, w_s, b_s_t)


def _merge_kernel(x_ref, a_ref, b_ref, c_ref, d_ref, gpre_ref, wg_ref, gb_ref, wup_ref, wout_ref, gpost_ref,
                  out_ref):
    x = x_ref[...]
    h = _rms(x, gpre_ref[...]).astype(BF16)
    merged = None
    for i, br in enumerate((a_ref, b_ref, c_ref, d_ref)):
        gate = _sigmoid(_dot(h, wg_ref[:, i * D_MODEL:(i + 1) * D_MODEL]) + gb_ref[i:i + 1, :])
        term = gate * _dot(br[...], wup_ref[i])
        merged = term if merged is None else merged + term
    y = _dot(merged.astype(BF16), wout_ref[...])
    out_ref[...] = x + _rms(y, gpost_ref[...])


def _merge(x, a, b, c_tb, d, gpre, wg, gb, wup, wout, gpost):
    B, S, D = x.shape
    tm = TOKEN_BLOCK
    xb = pl.BlockSpec((None, tm, D), lambda b_, i: (b_, i, 0))
    br = pl.BlockSpec((None, tm, W_BRANCH), lambda b_, i: (b_, i, 0))
    return pl.pallas_call(
        _merge_kernel,
        grid=(B, S // tm),
        in_specs=[
            xb, br, br,
            pl.BlockSpec((tm, W_BRANCH), lambda b_, i: (i, b_)),
            br,
            _const_spec((1, D)), _const_spec(wg.shape), _const_spec(gb.shape), _const_spec(wup.shape),
            _const_spec(wout.shape), _const_spec((1, D)),
        ],
        out_specs=xb,
        out_shape=jax.ShapeDtypeStruct((B, S, D), F32),
        compiler_params=_params("parallel", "parallel"),
        name="merge",
    )(x, a, b, c_tb, d, gpre, wg, gb, wup, wout, gpost)


def _mem_kv_kernel(mem_ref, g_ref, w_ref, k_ref, v_ref):
    mn = _rms(mem_ref[...], g_ref[...]).astype(BF16)
    hw = X_HEADS * X_HEAD_DIM
    k_ref[...] = _dot(mn, w_ref[:, :hw]).astype(BF16)
    v_ref[...] = _dot(mn, w_ref[:, hw:]).astype(BF16)


def _mem_kv(mem, g, w):
    B, M, D = mem.shape
    hw = X_HEADS * X_HEAD_DIM
    ob = pl.BlockSpec((None, M, hw), lambda b: (b, 0, 0))
    return pl.pallas_call(
        _mem_kv_kernel,
        grid=(B,),
        in_specs=[pl.BlockSpec((None, M, D), lambda b: (b, 0, 0)), _const_spec((1, D)), _const_spec(w.shape)],
        out_specs=[ob, ob],
        out_shape=[jax.ShapeDtypeStruct((B, M, hw), BF16)] * 2,
        compiler_params=_params("parallel"),
        name="mem_kv",
    )(mem, g, w)


def _xattn_kernel(x_ref, k_ref, v_ref, gpre_ref, wq_ref, wo_ref, gpost_ref, out_ref, o_ref):
    x = x_ref[...]
    h = _rms(x, gpre_ref[...]).astype(BF16)
    q = (_dot(h, wq_ref[...]) * (X_HEAD_DIM ** -0.5)).astype(BF16)
    for hd in range(X_HEADS):
        cols = slice(hd * X_HEAD_DIM, (hd + 1) * X_HEAD_DIM)
        s = lax.dot_general(q[:, cols], k_ref[:, cols], (((1,), (1,)), ((), ())), preferred_element_type=F32)
        m = jnp.max(s, axis=-1, keepdims=True)
        p = jnp.exp(s - m)
        l = jnp.sum(p, axis=-1, keepdims=True)
        o_ref[:, cols] = (_dot(p.astype(BF16), v_ref[:, cols]) * (1.0 / l)).astype(BF16)
    y = _dot(o_ref[...], wo_ref[...])
    out_ref[...] = x + _rms(y, gpost_ref[...])


def _xattn(x, k, v, gpre, wq, wo, gpost):
    B, S, D = x.shape
    tm = TOKEN_BLOCK
    hw = X_HEADS * X_HEAD_DIM
    xb = pl.BlockSpec((None, tm, D), lambda b, i: (b, i, 0))
    kb = pl.BlockSpec((None, N_MEM, hw), lambda b, i: (b, 0, 0))
    return pl.pallas_call(
        _xattn_kernel,
        grid=(B, S // tm),
        in_specs=[xb, kb, kb, _const_spec((1, D)), _const_spec(wq.shape), _const_spec(wo.shape),
                  _const_spec((1, D))],
        out_specs=xb,
        out_shape=jax.ShapeDtypeStruct((B, S, D), F32),
        scratch_shapes=[pltpu.VMEM((tm, hw), BF16)],
        compiler_params=_params("parallel", "parallel"),
        name="xattn",
    )(x, k, v, gpre, wq, wo, gpost)


def _ffn_kernel(x_ref, gpre_ref, w1_ref, w2_ref, gpost_ref, out_ref, acc_ref):
    x = x_ref[...]
    h = _rms(x, gpre_ref[...]).astype(BF16)
    for c in range(D_FF // FF_CHUNK):
        cols = slice(c * FF_CHUNK, (c + 1) * FF_CHUNK)
        a = jnp.maximum(_dot(h, w1_ref[:, cols]), 0.0)
        a = (a * a).astype(BF16)
        part = _dot(a, w2_ref[cols, :])
        if c == 0:
            acc_ref[...] = part
        else:
            acc_ref[...] += part
    out_ref[...] = x + _rms(acc_ref[...], gpost_ref[...])


def _ffn(x, gpre, w1, w2, gpost):
    B, S, D = x.shape
    tm = TOKEN_BLOCK
    xb = pl.BlockSpec((None, tm, D), lambda b, i: (b, i, 0))
    return pl.pallas_call(
        _ffn_kernel,
        grid=(B, S // tm),
        in_specs=[xb, _const_spec((1, D)), _const_spec(w1.shape), _const_spec(w2.shape), _const_spec((1, D))],
        out_specs=xb,
        out_shape=jax.ShapeDtypeStruct((B, S, D), F32),
        scratch_shapes=[pltpu.VMEM((tm, D), F32)],
        compiler_params=_params("parallel", "parallel"),
        name="ffn",
    )(x, gpre, w1, w2, gpost)


def _t5_bucket(n):
    exact = REL_BUCKETS // 2
    nf = np.maximum(n, 1).astype(np.float32)
    large = exact + (np.log(nf / exact) / np.log(REL_MAX_DIST / exact) * (REL_BUCKETS - exact)).astype(np.int32)
    large = np.minimum(large, REL_BUCKETS - 1)
    return np.where(n < exact, n, large).astype(np.int32)


def _band_bias(rel_bias, g, band, dil):
    d = np.arange(band + 1)
    table = rel_bias[jnp.asarray(_t5_bucket(d * dil))][:, g * ATT_HEADS:(g + 1) * ATT_HEADS].astype(F32)
    pad = jnp.full((ATT_HEADS, band - 1), NEG_INF, F32)
    f = jnp.concatenate([pad, table[::-1].T, pad], axis=1)
    return jnp.stack([f[:, band - 1 - i:3 * band - 1 - i] for i in range(band)], axis=1)


def kernel(x, mem, rel_bias, g_mix_pre, g_mix_post, w_in, gate_b, pool_w, pool_scale, a_re, a_im, log_dt,
           b_re, b_im, c_re, c_im, d_skip, w_glu, b_glu, sgu_ln_g, sgu_ln_b, w_s, b_s, w_up, w_out,
           g_x_pre, g_x_post, g_mem, w_cq, w_ckv, w_co, g_ff_pre, g_ff_post, w_ff1, w_ff2):
    B, S, D = x.shape
    depth = w_in.shape[0]
    assert D == D_MODEL and S % (TOKEN_BLOCK) == 0 and B == V7X_SUBLANES
    for win, dil in DIL_GROUPS:
        assert win // dil == ATT_BAND and (S // dil) % ATT_BAND == 0

    biases = [_band_bias(rel_bias, g, win // dil, dil) for g, (win, dil) in enumerate(DIL_GROUPS)]
    row = lambda a: a.reshape(1, -1).astype(F32)

    for l in range(depth):
        w_in_l = w_in[l].astype(BF16)
        pool_in, qkv1, qkv2, qkv3, ssm_tb, sgu_in = _in_proj(x, row(g_mix_pre[l]), w_in_l[:, :OFF_GATE])
        a_out = _pool(pool_in, pool_w[l].astype(BF16), row(pool_scale[l]))
        outs, lses = [], []
        for qkv, bias, (win, dil) in zip((qkv1, qkv2, qkv3), biases, DIL_GROUPS):
            o, lse = _attn_group(qkv, bias, dil)
            outs.append(o)
            lses.append(lse)
        b_out = _attn_mix(outs, lses)
        s5p = _s5_params(a_re[l], a_im[l], log_dt[l], b_re[l], b_im[l], c_re[l], c_im[l])
        c_tb = _s5(ssm_tb.reshape(S * B, W_BRANCH), B, *s5p, row(d_skip[l]), w_glu[l].astype(BF16), row(b_glu[l]))
        d_out = _sgu(sgu_in, row(sgu_ln_g[l]), row(sgu_ln_b[l]), w_s[l], b_s[l].T)
        x = _merge(x, a_out, b_out, c_tb.reshape(S, B * W_BRANCH), d_out, row(g_mix_pre[l]),
                   w_in_l[:, OFF_GATE:], gate_b[l], w_up[l].astype(BF16), w_out[l].astype(BF16),
                   row(g_mix_post[l]))
        k_mem, v_mem = _mem_kv(mem, row(g_mem[l]), w_ckv[l].astype(BF16))
        x = _xattn(x, k_mem, v_mem, row(g_x_pre[l]), w_cq[l].astype(BF16), w_co[l].astype(BF16),
                   row(g_x_post[l]))
        x = _ffn(x, row(g_ff_pre[l]), w_ff1[l].astype(BF16), w_ff2[l].astype(BF16), row(g_ff_post[l]))
    return x
```

```python
import functools
import math

import jax
import jax.numpy as jnp
import numpy as np
from jax import lax
from jax.experimental import pallas as pl
from jax.experimental.pallas import tpu as pltpu

F32 = jnp.float32
BF16 = jnp.bfloat16

D_MODEL = 1024
N_MEM = 256
N_BRANCH = 4
W_BRANCH = D_MODEL // 2
POOL_WINDOWS = (2, 4, 8, 16)
POOL_GROUP = W_BRANCH // len(POOL_WINDOWS)
DIL_GROUPS = ((128, 1), (512, 4), (2048, 16))
ATT_HEADS = 8
ATT_HEAD_DIM = W_BRANCH // ATT_HEADS
SSM_GROUP = 16
SSM_GROUPS = W_BRANCH // SSM_GROUP
SSM_STATE = 64
SSM_STATES = SSM_GROUPS * SSM_STATE
SGU_CHUNK = 128
SGU_GROUPS = 4
X_HEADS = 4
X_HEAD_DIM = 128
D_FF = 4 * D_MODEL
REL_BUCKETS = 32
REL_MAX_DIST = 2048
EPS = 1e-6
NEG_INF = -1e30
N_ATT_COLS = 3 * len(DIL_GROUPS) * W_BRANCH
OFF_GATE = W_BRANCH + N_ATT_COLS + W_BRANCH + 2 * W_BRANCH

V7X_LANES = 128
V7X_SUBLANES = 8
V7X_MXU_DIM = 256
V7X_VMEM_LIMIT_BYTES = 56 * 1024 * 1024

TOKEN_BLOCK = 512
POOL_HALO = 16
ATT_BAND = 128
ATT_SPAN = 512
ATT_UNITS = 4
LOG2E = 1.4426950408889634
ATT_Q_SCALE = ATT_HEAD_DIM ** -0.5 * LOG2E
S5_STEPS = 64
S5_PITCH = S5_STEPS + 8
S5_LANE_CHUNK = 512
FF_CHUNK = 512


def _params(*sem):
    return pltpu.CompilerParams(dimension_semantics=sem, vmem_limit_bytes=V7X_VMEM_LIMIT_BYTES)


def _const_spec(shape):
    nd = len(shape)
    return pl.BlockSpec(shape, lambda *_: (0,) * nd, pipeline_mode=pl.Buffered(1))


def _rms(xf, g):
    return xf * lax.rsqrt(jnp.mean(xf * xf, axis=-1, keepdims=True) + EPS) * g


def _gelu(x):
    c = math.sqrt(2.0 / math.pi)
    return 0.5 * x * (1.0 + jnp.tanh(c * (x + 0.044715 * (x * x * x))))


def _sigmoid(x):
    return 1.0 / (1.0 + jnp.exp(-x))


def _dot(a, b):
    return jnp.dot(a, b, preferred_element_type=F32)


def _in_proj_kernel(x_ref, g_ref, w_ref, pool_ref, qkv1_ref, qkv2_ref, qkv3_ref, ssm_ref, sgu_ref, slab_ref):
    h = _rms(x_ref[...], g_ref[...]).astype(BF16)
    tm = x_ref.shape[0]
    step = W_BRANCH
    ng = len(DIL_GROUPS)
    nslab = step // V7X_LANES
    pool_ref[...] = _dot(h, w_ref[:, 0:step]).astype(BF16)
    for which in range(3):
        dst = slice(which * step, (which + 1) * step)
        for g, out_ref in enumerate((qkv1_ref, qkv2_ref, qkv3_ref)):
            lo = W_BRANCH + (which * ng + g) * step
            y = _dot(h, w_ref[:, lo:lo + step])
            if which == 0:
                y = y * ATT_Q_SCALE
            dil = DIL_GROUPS[g][1]
            if dil == 1:
                out_ref[:, dst] = y.astype(BF16)
                continue
            for c in range(nslab):
                slab_ref[c] = y[:, c * V7X_LANES:(c + 1) * V7X_LANES]
            for c in range(nslab):
                cols = slice(which * step + c * V7X_LANES, which * step + (c + 1) * V7X_LANES)
                for r in range(dil):
                    out_ref[r, :, cols] = slab_ref[c, pl.ds(r, tm // dil, stride=dil), :].astype(BF16)
    lo = W_BRANCH + N_ATT_COLS
    ssm_ref[...] = _dot(h, w_ref[:, lo:lo + step]).astype(BF16)
    lo += W_BRANCH
    for j in range(2):
        sgu_ref[:, j * step:(j + 1) * step] = _dot(h, w_ref[:, lo + j * step:lo + (j + 1) * step]).astype(BF16)


def _residue_major_shape(B, S, dil, width):
    return (B, S // ATT_SPAN, dil, ATT_SPAN // dil, width)


def _in_proj(x, g, w):
    B, S, D = x.shape
    tm = ATT_SPAN
    qw = 3 * W_BRANCH
    d2, d3 = DIL_GROUPS[1][1], DIL_GROUPS[2][1]
    return pl.pallas_call(
        _in_proj_kernel,
        grid=(B, S // tm),
        in_specs=[
            pl.BlockSpec((None, tm, D), lambda b, i: (b, i, 0)),
            _const_spec((1, D)),
            _const_spec((D, OFF_GATE)),
        ],
        out_specs=[
            pl.BlockSpec((None, tm, W_BRANCH), lambda b, i: (b, i, 0)),
            pl.BlockSpec((None, tm, qw), lambda b, i: (b, i, 0)),
            pl.BlockSpec((None, None, d2, tm // d2, qw), lambda b, i: (b, i, 0, 0, 0)),
            pl.BlockSpec((None, None, d3, tm // d3, qw), lambda b, i: (b, i, 0, 0, 0)),
            pl.BlockSpec((None, tm, W_BRANCH), lambda b, i: (b, i, 0)),
            pl.BlockSpec((None, tm, 2 * W_BRANCH), lambda b, i: (b, i, 0)),
        ],
        out_shape=[
            jax.ShapeDtypeStruct((B, S, W_BRANCH), BF16),
            jax.ShapeDtypeStruct((B, S, qw), BF16),
            jax.ShapeDtypeStruct(_residue_major_shape(B, S, d2, qw), BF16),
            jax.ShapeDtypeStruct(_residue_major_shape(B, S, d3, qw), BF16),
            jax.ShapeDtypeStruct((B, S, W_BRANCH), BF16),
            jax.ShapeDtypeStruct((B, S, 2 * W_BRANCH), BF16),
        ],
        scratch_shapes=[pltpu.VMEM((W_BRANCH // V7X_LANES, tm, V7X_LANES), F32)],
        compiler_params=_params("parallel", "parallel"),
        name="in_proj",
    )(x, g, w)


def _pool_kernel(cur_ref, halo_ref, w_ref, scale_ref, out_ref, ext_ref):
    i = pl.program_id(1)
    tb = cur_ref.shape[0]
    halo = halo_ref[...].astype(F32)
    ext_ref[0:POOL_HALO, :] = jnp.where(i == 0, 0.0, halo)
    ext_ref[POOL_HALO:, :] = cur_ref[...].astype(F32)
    tpos = i * tb + lax.broadcasted_iota(jnp.int32, (tb, 1), 0)
    for gi, win in enumerate(POOL_WINDOWS):
        lanes = slice(gi * POOL_GROUP, (gi + 1) * POOL_GROUP)
        tok = ext_ref[POOL_HALO:POOL_HALO + tb, lanes]
        s = tok
        for j in range(1, win):
            s = s + ext_ref[POOL_HALO - j:POOL_HALO - j + tb, lanes]
        cnt = jnp.minimum(tpos + 1, win).astype(F32)
        p = s / cnt - tok
        y = _dot(p.astype(BF16), w_ref[gi])
        out_ref[:, lanes] = (y * scale_ref[:, lanes]).astype(BF16)


def _pool(pool_in, w_pool, scale):
    B, S, W = pool_in.shape
    tb = TOKEN_BLOCK
    hb = tb // POOL_HALO
    return pl.pallas_call(
        _pool_kernel,
        grid=(B, S // tb),
        in_specs=[
            pl.BlockSpec((None, tb, W), lambda b, i: (b, i, 0)),
            pl.BlockSpec((None, POOL_HALO, W), lambda b, i: (b, jnp.maximum(i * hb - 1, 0), 0)),
            _const_spec(w_pool.shape),
            _const_spec((1, W)),
        ],
        out_specs=pl.BlockSpec((None, tb, W), lambda b, i: (b, i, 0)),
        out_shape=jax.ShapeDtypeStruct((B, S, W), BF16),
        scratch_shapes=[pltpu.VMEM((POOL_HALO + tb, W), F32)],
        compiler_params=_params("parallel", "parallel"),
        name="pool",
    )(pool_in, pool_in, w_pool, scale)


def _attn_kernel(layout, q_ref, kp_ref, kc_ref, vp_ref, vc_ref, bias_ref, o_ref, st_ref):
    band, W = ATT_BAND, W_BRANCH
    lane = lax.broadcasted_iota(jnp.int32, (1, V7X_LANES), 1)
    low = lane < ATT_HEAD_DIM
    first_step = pl.program_id(1) == 0
    if layout == "natural":
        k_ext = jnp.concatenate([kp_ref[...], kc_ref[...]], axis=0)
        v_ext = jnp.concatenate([vp_ref[...], vc_ref[...]], axis=0)

    def unit(u):
        if layout == "natural":
            rows = slice(u * band, (u + 1) * band)
            table = jnp.where(first_step, 1, 0) if u == 0 else 0
            return (q_ref[rows, :], k_ext[u * band:(u + 2) * band], v_ext[u * band:(u + 2) * band], table,
                    lambda cols, val: o_ref.__setitem__((rows, cols), val),
                    lambda val: st_ref.__setitem__((rows, slice(None)), val))
        table = jnp.where(first_step, 1, 0)
        if layout == "span":
            cat = lambda p, c: jnp.concatenate([p[u], c[u]], axis=0)
            return (q_ref[u], cat(kp_ref, kc_ref), cat(vp_ref, vc_ref), table,
                    lambda cols, val: o_ref.__setitem__((u, slice(None), cols), val),
                    lambda val: st_ref.__setitem__((u,), val))
        sp, per = q_ref.shape[0], q_ref.shape[2]
        get = lambda ref: ref[:, u].reshape(band, W)
        cat = lambda p, c: jnp.concatenate([get(p), get(c)], axis=0)
        return (get(q_ref), cat(kp_ref, kc_ref), cat(vp_ref, vc_ref), table,
                lambda cols, val: o_ref.__setitem__((slice(None), u, slice(None), cols),
                                                    val.reshape(sp, per, V7X_LANES)),
                lambda val: st_ref.__setitem__((slice(None), u), val.reshape(sp, per, V7X_LANES)))

    for u in range(ATT_UNITS):
        q, k, v, table, put_o, put_st = unit(u)
        stats = jnp.zeros((band, V7X_LANES), F32)
        for j in range(ATT_HEADS // 2):
            cols = slice(j * V7X_LANES, (j + 1) * V7X_LANES)
            qp, kp, vp = q[:, cols], k[:, cols], v[:, cols]
            zq, zv = jnp.zeros_like(qp), jnp.zeros_like(vp)
            q2 = jnp.concatenate([jnp.where(low, qp, zq), jnp.where(low, zq, qp)], axis=0)
            s = lax.dot_general(q2, kp, (((1,), (1,)), ((), ())), preferred_element_type=F32)
            s = s + jnp.concatenate([bias_ref[table, 2 * j], bias_ref[table, 2 * j + 1]], axis=0)
            m = jnp.max(s, axis=-1, keepdims=True)
            p = jnp.exp2(s - m)
            l = jnp.sum(p, axis=-1, keepdims=True)
            pb = p.astype(BF16)
            p2 = jnp.concatenate([pb[:band], pb[band:]], axis=1)
            v2 = jnp.concatenate([jnp.where(low, vp, zv), jnp.where(low, zv, vp)], axis=0)
            put_o(cols, _dot(p2, v2).astype(BF16))
            for e in range(2):
                h = 2 * j + e
                stats = jnp.where(lane == h, m[e * band:(e + 1) * band], stats)
                stats = jnp.where(lane == ATT_HEADS + h, l[e * band:(e + 1) * band], stats)
        put_st(stats)


def _attn_group(qkv, bias, dil):
    B = qkv.shape[0]
    W = W_BRANCH
    U = ATT_UNITS
    if dil == 1:
        S = qkv.shape[1]
        layout, grid = "natural", (B, S // (U * ATT_BAND))
        blk = lambda width: (None, U * ATT_BAND, width)
        prev_blk = (None, ATT_BAND, W)
        idx = lambda which: (lambda b, i: (b, i, which))
        idx_prev = lambda which: (lambda b, i: (b, jnp.maximum(U * i - 1, 0), which))
        out_shape = lambda width: (B, S, width)
        sem = ("parallel", "parallel")
    else:
        nspan, _, per, _ = qkv.shape[1:]
        spans = ATT_BAND // per
        if spans == 1:
            assert dil == U
            layout, grid = "span", (B, nspan)
            blk = lambda width: (None, None, U, per, width)
            idx = lambda which: (lambda b, n: (b, n, 0, 0, which))
            idx_prev = lambda which: (lambda b, n: (b, jnp.maximum(n - 1, 0), 0, 0, which))
            sem = ("parallel", "parallel")
        else:
            layout, grid = "band", (B, nspan // spans, dil // U)
            blk = lambda width: (None, spans, U, per, width)
            idx = lambda which: (lambda b, n, r: (b, n, r, 0, which))
            idx_prev = lambda which: (lambda b, n, r: (b, jnp.maximum(n - 1, 0), r, 0, which))
            sem = ("parallel", "parallel", "parallel")
        prev_blk = blk(W)
        out_shape = lambda width: (B, nspan, dil, per, width)
    return pl.pallas_call(
        functools.partial(_attn_kernel, layout),
        grid=grid,
        in_specs=[
            pl.BlockSpec(blk(W), idx(0)),
            pl.BlockSpec(prev_blk, idx_prev(1)),
            pl.BlockSpec(blk(W), idx(1)),
            pl.BlockSpec(prev_blk, idx_prev(2)),
            pl.BlockSpec(blk(W), idx(2)),
            _const_spec(bias.shape),
        ],
        out_specs=[
            pl.BlockSpec(blk(W), idx(0)),
            pl.BlockSpec(blk(V7X_LANES), idx(0)),
        ],
        out_shape=[
            jax.ShapeDtypeStruct(out_shape(W), BF16),
            jax.ShapeDtypeStruct(out_shape(V7X_LANES), F32),
        ],
        compiler_params=_params(*sem),
        name=f"attn_d{dil}",
    )(qkv, qkv, qkv, qkv, qkv, bias)


def _attn_mix_kernel(o1_ref, o2_ref, o3_ref, s1_ref, s2_ref, s3_ref, out_ref, on_ref, sn_ref):
    lane = lax.broadcasted_iota(jnp.int32, (1, V7X_LANES), 1)
    low = lane < ATT_HEAD_DIM
    nslab = W_BRANCH // V7X_LANES
    for gi, (o_ref, s_ref) in enumerate(((o2_ref, s2_ref), (o3_ref, s3_ref))):
        dil, per = o_ref.shape[0], o_ref.shape[1]
        for r in range(dil):
            sn_ref[gi, pl.ds(r, per, stride=dil), :] = s_ref[r]
            for c in range(nslab):
                on_ref[gi * nslab + c, pl.ds(r, per, stride=dil), :] = (
                    o_ref[r, :, c * V7X_LANES:(c + 1) * V7X_LANES].astype(F32))
    stats = (s1_ref[...], sn_ref[0], sn_ref[1])
    for j in range(ATT_HEADS // 2):
        cols = slice(j * V7X_LANES, (j + 1) * V7X_LANES)
        wts = []
        for e in range(2):
            h = 2 * j + e
            ms = [x[:, h:h + 1] for x in stats]
            ls = [x[:, ATT_HEADS + h:ATT_HEADS + h + 1] for x in stats]
            top = jnp.maximum(jnp.maximum(ms[0], ms[1]), ms[2])
            ex = [jnp.exp2(x - top) for x in ms]
            inv = 1.0 / (ex[0] * ls[0] + ex[1] * ls[1] + ex[2] * ls[2])
            wts.append([x * inv for x in ex])
        vals = (o1_ref[:, cols].astype(F32), on_ref[j], on_ref[nslab + j])
        acc = None
        for gi in range(3):
            term = jnp.where(low, wts[0][gi], wts[1][gi]) * vals[gi]
            acc = term if acc is None else acc + term
        out_ref[:, cols] = acc.astype(BF16)


def _attn_mix(os_, stats):
    B, S, W = os_[0].shape
    tm = ATT_SPAN

    def spec(a):
        if a.ndim == 3:
            return pl.BlockSpec((None, tm, a.shape[-1]), lambda b, i: (b, i, 0))
        return pl.BlockSpec((None, None) + a.shape[2:], lambda b, i: (b, i, 0, 0, 0))

    return pl.pallas_call(
        _attn_mix_kernel,
        grid=(B, S // tm),
        in_specs=[spec(a) for a in (*os_, *stats)],
        out_specs=pl.BlockSpec((None, tm, W), lambda b, i: (b, i, 0)),
        out_shape=jax.ShapeDtypeStruct((B, S, W), BF16),
        scratch_shapes=[
            pltpu.VMEM((2 * W // V7X_LANES, tm, V7X_LANES), F32),
            pltpu.VMEM((2, tm, V7X_LANES), F32),
        ],
        compiler_params=_params("parallel", "parallel"),
        name="attn_mix",
    )(*os_, *stats)


def _s5_kernel(u_ref, wbr_ref, wbi_ref, ar_ref, ai_ref, cr_ref, ci_ref, dsk_ref, wglu_ref, bglu_ref,
               out_ref, h_ref, slab_ref, g_ref):
    nst = SSM_STATES
    nb, tc, W = u_ref.shape
    pitch = S5_PITCH
    lanes = V7X_LANES
    nslab = nst // lanes

    @pl.when(pl.program_id(0) == 0)
    def _():
        h_ref[...] = jnp.zeros_like(h_ref)

    u = u_ref[...].reshape(nb * tc, W)

    tile = V7X_MXU_DIM
    chan_tile_states = tile * SSM_STATE // SSM_GROUP
    for j in range(nst // tile):
        kt = (j * tile // chan_tile_states) * tile
        uk = u[:, kt:kt + tile]
        for part, w_ref in enumerate((wbr_ref, wbi_ref)):
            y = _dot(uk, w_ref[kt:kt + tile, j * tile:(j + 1) * tile])
            for half in range(tile // lanes):
                sl = part * nslab + j * (tile // lanes) + half
                for b in range(nb):
                    slab_ref[sl, b * pitch:b * pitch + tc, :] = y[b * tc:(b + 1) * tc, half * lanes:(half + 1) * lanes]

    per = S5_LANE_CHUNK // lanes
    for c in range(nslab // per):
        sl_r = [c * per + s for s in range(per)]
        sl_i = [nslab + c * per + s for s in range(per)]
        ar = [jnp.broadcast_to(ar_ref[:, s * lanes:(s + 1) * lanes], (nb, lanes)) for s in sl_r]
        ai = [jnp.broadcast_to(ai_ref[:, s * lanes:(s + 1) * lanes], (nb, lanes)) for s in sl_r]

        def body(t, carry, sl_r=sl_r, sl_i=sl_i, ar=ar, ai=ai):
            hr, hi = carry
            nr, ni = [], []
            for s in range(per):
                rows = pl.ds(t, nb, stride=pitch)
                br = slab_ref[sl_r[s], rows, :]
                bi = slab_ref[sl_i[s], rows, :]
                r_new = ar[s] * hr[s] - ai[s] * hi[s] + br
                i_new = ar[s] * hi[s] + ai[s] * hr[s] + bi
                slab_ref[sl_r[s], rows, :] = r_new
                slab_ref[sl_i[s], rows, :] = i_new
                nr.append(r_new)
                ni.append(i_new)
            return tuple(nr), tuple(ni)

        init = (tuple(h_ref[:, s * lanes:(s + 1) * lanes] for s in sl_r),
                tuple(h_ref[:, s * lanes:(s + 1) * lanes] for s in sl_i))
        hr, hi = lax.fori_loop(0, tc, body, init, unroll=4)
        for k, s in enumerate(sl_r):
            h_ref[:, s * lanes:(s + 1) * lanes] = hr[k]
        for k, s in enumerate(sl_i):
            h_ref[:, s * lanes:(s + 1) * lanes] = hi[k]

    def states(first_slab, count):
        return jnp.concatenate(
            [jnp.concatenate([slab_ref[first_slab + s, b * pitch:b * pitch + tc, :] for b in range(nb)], axis=0)
             for s in range(count)], axis=1).astype(BF16)

    cnt = chan_tile_states // lanes
    for n in range(W_BRANCH // tile):
        s0 = n * chan_tile_states
        cols = slice(n * tile, (n + 1) * tile)
        yr = _dot(states(n * cnt, cnt), cr_ref[s0:s0 + chan_tile_states, cols])
        yi = _dot(states(nslab + n * cnt, cnt), ci_ref[s0:s0 + chan_tile_states, cols])
        y = yr - yi + u[:, cols].astype(F32) * dsk_ref[:, cols]
        g_ref[:, cols] = _gelu(y)
    g = g_ref[...]
    z = _dot(g.astype(BF16), wglu_ref[...]) + bglu_ref[...]
    out_ref[...] = (g * _sigmoid(z)).astype(BF16).reshape(nb, tc, W)


def _s5(u, wbr, wbi, ar, ai, cr, ci, dskip, wglu, bglu):
    B, S, W = u.shape
    tc = S5_STEPS
    blk = pl.BlockSpec((B, tc, W), lambda k: (0, k, 0))
    return pl.pallas_call(
        _s5_kernel,
        grid=(S // tc,),
        in_specs=[blk] + [_const_spec(a.shape) for a in (wbr, wbi, ar, ai, cr, ci, dskip, wglu, bglu)],
        out_specs=blk,
        out_shape=jax.ShapeDtypeStruct((B, S, W), BF16),
        scratch_shapes=[
            pltpu.VMEM((B, 2 * SSM_STATES), F32),
            pltpu.VMEM((2 * SSM_STATES // V7X_LANES, B * S5_PITCH, V7X_LANES), F32),
            pltpu.VMEM((B * tc, W), F32),
        ],
        compiler_params=_params("arbitrary"),
        name="s5",
    )(u, wbr, wbi, ar, ai, cr, ci, dskip, wglu, bglu)


def _s5_params(a_re, a_im, log_dt, b_re, b_im, c_re, c_im):
    G, P, C = SSM_GROUPS, SSM_STATE, SSM_GROUP
    lam_re = jnp.minimum(a_re, -1e-4)
    lam_im = a_im
    dt = jnp.exp(log_dt)[:, None]
    mag = jnp.exp(lam_re * dt)
    ab_re, ab_im = mag * jnp.cos(lam_im * dt), mag * jnp.sin(lam_im * dt)
    den = lam_re * lam_re + lam_im * lam_im
    f_re = ((ab_re - 1.0) * lam_re + ab_im * lam_im) / den
    f_im = (ab_im * lam_re - (ab_re - 1.0) * lam_im) / den
    bb_re = f_re[..., None] * b_re - f_im[..., None] * b_im
    bb_im = f_re[..., None] * b_im + f_im[..., None] * b_re
    eye = jnp.eye(G, dtype=F32)

    def in_mat(bb):
        return jnp.einsum('gpc,gh->gchp', bb, eye).reshape(G * C, G * P).astype(BF16)

    def out_mat(cc):
        return jnp.einsum('gcp,gh->gphc', cc, eye).reshape(G * P, G * C).astype(BF16)

    return (in_mat(bb_re), in_mat(bb_im), ab_re.reshape(1, G * P), ab_im.reshape(1, G * P),
            out_mat(c_re), out_mat(c_im))


def _sgu_kernel(z_ref, lng_ref, lnb_ref, ws_ref, bst_ref, out_ref):
    tb = z_ref.shape[0]
    T = SGU_CHUNK
    gd = W_BRANCH // SGU_GROUPS
    z = _gelu(z_ref[...].astype(F32))
    u = z[:, :W_BRANCH]
    v = z[:, W_BRANCH:]
    mu = jnp.mean(v, axis=-1, keepdims=True)
    vc = v - mu
    var = jnp.mean(vc * vc, axis=-1, keepdims=True)
    vn = (vc * lax.rsqrt(var + EPS) * lng_ref[...] + lnb_ref[...]).astype(BF16)
    causal = lax.broadcasted_iota(jnp.int32, (T, T), 0) >= lax.broadcasted_iota(jnp.int32, (T, T), 1)
    for g in range(SGU_GROUPS):
        wg = jnp.where(causal, ws_ref[g], 0.0).astype(BF16)
        bias = bst_ref[:, g:g + 1]
        for c in range(tb // T):
            rows = slice(c * T, (c + 1) * T)
            cols = slice(g * gd, (g + 1) * gd)
            sv = _dot(wg, vn[rows, cols]) + bias
            out_ref[rows, cols] = (u[rows, cols] * sv).astype(BF16)


def _sgu(z, ln_g, ln_b, w_s, b_s_t):
    B, S, W2 = z.shape
    tb = TOKEN_BLOCK
    return pl.pallas_call(
        _sgu_kernel,
        grid=(B, S // tb),
        in_specs=[
            pl.BlockSpec((None, tb, W2), lambda b, i: (b, i, 0)),
            _const_spec((1, W_BRANCH)),
            _const_spec((1, W_BRANCH)),
            _const_spec(w_s.shape),
            _const_spec(b_s_t.shape),
        ],
        out_specs=pl.BlockSpec((None, tb, W_BRANCH), lambda b, i: (b, i, 0)),
        out_shape=jax.ShapeDtypeStruct((B, S, W_BRANCH), BF16),
        compiler_params=_params("parallel", "parallel"),
        name="sgu",
    )(z, ln_g, ln_b, w_s, b_s_t)


def _merge_kernel(x_ref, a_ref, b_ref, c_ref, d_ref, gpre_ref, wg_ref, gb_ref, wup_ref, wout_ref, gpost_ref,
                  out_ref):
    x = x_ref[...]
    h = _rms(x, gpre_ref[...]).astype(BF16)
    merged = None
    for i, br in enumerate((a_ref, b_ref, c_ref, d_ref)):
        gate = _sigmoid(_dot(h, wg_ref[:, i * D_MODEL:(i + 1) * D_MODEL]) + gb_ref[i:i + 1, :])
        term = gate * _dot(br[...], wup_ref[i])
        merged = term if merged is None else merged + term
    y = _dot(merged.astype(BF16), wout_ref[...])
    out_ref[...] = x + _rms(y, gpost_ref[...])


def _merge(x, a, b, c, d, gpre, wg, gb, wup, wout, gpost):
    B, S, D = x.shape
    tm = TOKEN_BLOCK
    xb = pl.BlockSpec((None, tm, D), lambda b_, i: (b_, i, 0))
    br = pl.BlockSpec((None, tm, W_BRANCH), lambda b_, i: (b_, i, 0))
    return pl.pallas_call(
        _merge_kernel,
        grid=(B, S // tm),
        in_specs=[
            xb, br, br, br, br,
            _const_spec((1, D)), _const_spec(wg.shape), _const_spec(gb.shape), _const_spec(wup.shape),
            _const_spec(wout.shape), _const_spec((1, D)),
        ],
        out_specs=xb,
        out_shape=jax.ShapeDtypeStruct((B, S, D), F32),
        compiler_params=_params("parallel", "parallel"),
        name="merge",
    )(x, a, b, c, d, gpre, wg, gb, wup, wout, gpost)


def _mem_kv_kernel(mem_ref, g_ref, w_ref, k_ref, v_ref):
    mn = _rms(mem_ref[...], g_ref[...]).astype(BF16)
    hw = X_HEADS * X_HEAD_DIM
    k_ref[...] = _dot(mn, w_ref[:, :hw]).astype(BF16)
    v_ref[...] = _dot(mn, w_ref[:, hw:]).astype(BF16)


def _mem_kv(mem, g, w):
    B, M, D = mem.shape
    hw = X_HEADS * X_HEAD_DIM
    ob = pl.BlockSpec((None, M, hw), lambda b: (b, 0, 0))
    return pl.pallas_call(
        _mem_kv_kernel,
        grid=(B,),
        in_specs=[pl.BlockSpec((None, M, D), lambda b: (b, 0, 0)), _const_spec((1, D)), _const_spec(w.shape)],
        out_specs=[ob, ob],
        out_shape=[jax.ShapeDtypeStruct((B, M, hw), BF16)] * 2,
        compiler_params=_params("parallel"),
        name="mem_kv",
    )(mem, g, w)


def _xattn_ffn_kernel(x_ref, k_ref, v_ref, gxpre_ref, wq_ref, wo_ref, gxpost_ref, gfpre_ref, w1_ref, w2_ref,
                      gfpost_ref, out_ref, o_ref, acc_ref):
    x = x_ref[...]
    h = _rms(x, gxpre_ref[...]).astype(BF16)
    q = (_dot(h, wq_ref[...]) * (X_HEAD_DIM ** -0.5)).astype(BF16)
    for hd in range(X_HEADS):
        cols = slice(hd * X_HEAD_DIM, (hd + 1) * X_HEAD_DIM)
        s = lax.dot_general(q[:, cols], k_ref[:, cols], (((1,), (1,)), ((), ())), preferred_element_type=F32)
        m = jnp.max(s, axis=-1, keepdims=True)
        p = jnp.exp(s - m)
        l = jnp.sum(p, axis=-1, keepdims=True)
        o_ref[:, cols] = (_dot(p.astype(BF16), v_ref[:, cols]) * (1.0 / l)).astype(BF16)
    x = x + _rms(_dot(o_ref[...], wo_ref[...]), gxpost_ref[...])
    h = _rms(x, gfpre_ref[...]).astype(BF16)
    for c in range(D_FF // FF_CHUNK):
        cols = slice(c * FF_CHUNK, (c + 1) * FF_CHUNK)
        a = jnp.maximum(_dot(h, w1_ref[:, cols]), 0.0)
        a = (a * a).astype(BF16)
        part = _dot(a, w2_ref[cols, :])
        if c == 0:
            acc_ref[...] = part
        else:
            acc_ref[...] += part
    out_ref[...] = x + _rms(acc_ref[...], gfpost_ref[...])


def _xattn_ffn(x, k, v, gxpre, wq, wo, gxpost, gfpre, w1, w2, gfpost):
    B, S, D = x.shape
    tm = TOKEN_BLOCK
    hw = X_HEADS * X_HEAD_DIM
    xb = pl.BlockSpec((None, tm, D), lambda b, i: (b, i, 0))
    kb = pl.BlockSpec((None, N_MEM, hw), lambda b, i: (b, 0, 0))
    consts = (gxpre, wq, wo, gxpost, gfpre, w1, w2, gfpost)
    return pl.pallas_call(
        _xattn_ffn_kernel,
        grid=(B, S // tm),
        in_specs=[xb, kb, kb] + [_const_spec(a.shape) for a in consts],
        out_specs=xb,
        out_shape=jax.ShapeDtypeStruct((B, S, D), F32),
        scratch_shapes=[pltpu.VMEM((tm, hw), BF16), pltpu.VMEM((tm, D), F32)],
        compiler_params=_params("parallel", "parallel"),
        name="xattn_ffn",
    )(x, k, v, *consts)


def _t5_bucket(n):
    exact = REL_BUCKETS // 2
    nf = np.maximum(n, 1).astype(np.float32)
    large = exact + (np.log(nf / exact) / np.log(REL_MAX_DIST / exact) * (REL_BUCKETS - exact)).astype(np.int32)
    large = np.minimum(large, REL_BUCKETS - 1)
    return np.where(n < exact, n, large).astype(np.int32)


def _band_bias(rel_bias, g, band, dil):
    d = np.arange(band + 1)
    table = rel_bias[jnp.asarray(_t5_bucket(d * dil))][:, g * ATT_HEADS:(g + 1) * ATT_HEADS].astype(F32) * LOG2E
    pad = jnp.full((ATT_HEADS, band - 1), NEG_INF, F32)
    f = jnp.concatenate([pad, table[::-1].T, pad], axis=1)
    full = jnp.stack([f[:, band - 1 - i:3 * band - 1 - i] for i in range(band)], axis=1)
    no_prev = jnp.concatenate([jnp.full((ATT_HEADS, band, band), NEG_INF, F32), full[:, :, band:]], axis=2)
    return jnp.stack([full, no_prev], axis=0)


def kernel(x, mem, rel_bias, g_mix_pre, g_mix_post, w_in, gate_b, pool_w, pool_scale, a_re, a_im, log_dt,
           b_re, b_im, c_re, c_im, d_skip, w_glu, b_glu, sgu_ln_g, sgu_ln_b, w_s, b_s, w_up, w_out,
           g_x_pre, g_x_post, g_mem, w_cq, w_ckv, w_co, g_ff_pre, g_ff_post, w_ff1, w_ff2):
    B, S, D = x.shape
    depth = w_in.shape[0]
    assert D == D_MODEL and S % (TOKEN_BLOCK) == 0 and B == V7X_SUBLANES
    for win, dil in DIL_GROUPS:
        assert win // dil == ATT_BAND and (S // dil) % ATT_BAND == 0

    biases = [_band_bias(rel_bias, g, win // dil, dil) for g, (win, dil) in enumerate(DIL_GROUPS)]
    row = lambda a: a.reshape(1, -1).astype(F32)

    for l in range(depth):
        w_in_l = w_in[l].astype(BF16)
        pool_in, qkv1, qkv2, qkv3, ssm_in, sgu_in = _in_proj(x, row(g_mix_pre[l]), w_in_l[:, :OFF_GATE])
        a_out = _pool(pool_in, pool_w[l].astype(BF16), row(pool_scale[l]))
        outs, stats = [], []
        for qkv, bias, (win, dil) in zip((qkv1, qkv2, qkv3), biases, DIL_GROUPS):
            o, st = _attn_group(qkv, bias, dil)
            outs.append(o)
            stats.append(st)
        b_out = _attn_mix(outs, stats)
        s5p = _s5_params(a_re[l], a_im[l], log_dt[l], b_re[l], b_im[l], c_re[l], c_im[l])
        c_out = _s5(ssm_in, *s5p, row(d_skip[l]), w_glu[l].astype(BF16), row(b_glu[l]))
        d_out = _sgu(sgu_in, row(sgu_ln_g[l]), row(sgu_ln_b[l]), w_s[l], b_s[l].T)
        x = _merge(x, a_out, b_out, c_out, d_out, row(g_mix_pre[l]),
                   w_in_l[:, OFF_GATE:], gate_b[l], w_up[l].astype(BF16), w_out[l].astype(BF16),
                   row(g_mix_post[l]))
        k_mem, v_mem = _mem_kv(mem, row(g_mem[l]), w_ckv[l].astype(BF16))
        x = _xattn_ffn(x, k_mem, v_mem, row(g_x_pre[l]), w_cq[l].astype(BF16), w_co[l].astype(BF16),
                       row(g_x_post[l]), row(g_ff_pre[l]), w_ff1[l].astype(BF16), w_ff2[l].astype(BF16),
                       row(g_ff_post[l]))
    return x
```

```python
import functools
import math

import jax
import jax.numpy as jnp
import numpy as np
from jax import lax
from jax.experimental import pallas as pl
from jax.experimental.pallas import tpu as pltpu

F32 = jnp.float32
BF16 = jnp.bfloat16

D_MODEL = 1024
N_MEM = 256
N_BRANCH = 4
W_BRANCH = D_MODEL // 2
POOL_WINDOWS = (2, 4, 8, 16)
POOL_GROUP = W_BRANCH // len(POOL_WINDOWS)
DIL_GROUPS = ((128, 1), (512, 4), (2048, 16))
ATT_HEADS = 8
ATT_HEAD_DIM = W_BRANCH // ATT_HEADS
SSM_GROUP = 16
SSM_GROUPS = W_BRANCH // SSM_GROUP
SSM_STATE = 64
SSM_STATES = SSM_GROUPS * SSM_STATE
SGU_CHUNK = 128
SGU_GROUPS = 4
X_HEADS = 4
X_HEAD_DIM = 128
D_FF = 4 * D_MODEL
REL_BUCKETS = 32
REL_MAX_DIST = 2048
EPS = 1e-6
NEG_INF = -1e30
N_ATT_COLS = 3 * len(DIL_GROUPS) * W_BRANCH
OFF_GATE = W_BRANCH + N_ATT_COLS + W_BRANCH + 2 * W_BRANCH

V7X_LANES = 128
V7X_SUBLANES = 8
V7X_MXU_DIM = 256
V7X_VMEM_LIMIT_BYTES = 56 * 1024 * 1024

TOKEN_BLOCK = 512
POOL_HALO = 16
ATT_BAND = 128
ATT_SPAN = 512
ATT_UNITS = 4
IN_PROJ_SLAB_SETS = 3
LOG2E = 1.4426950408889634
ATT_Q_SCALE = ATT_HEAD_DIM ** -0.5 * LOG2E
S5_STEPS = 64
S5_PITCH = S5_STEPS + 8
S5_LANE_CHUNK = 512
FF_CHUNK = 512


def _params(*sem):
    return pltpu.CompilerParams(dimension_semantics=sem, vmem_limit_bytes=V7X_VMEM_LIMIT_BYTES)


def _const_spec(shape):
    nd = len(shape)
    return pl.BlockSpec(shape, lambda *_: (0,) * nd, pipeline_mode=pl.Buffered(1))


def _layer_spec(shape, l):
    nd = len(shape) - 1
    return pl.BlockSpec((None,) + tuple(shape[1:]), lambda *_: (l,) + (0,) * nd, pipeline_mode=pl.Buffered(1))


def _rms(xf, g):
    return xf * lax.rsqrt(jnp.mean(xf * xf, axis=-1, keepdims=True) + EPS) * g


def _gelu(x):
    c = math.sqrt(2.0 / math.pi)
    return 0.5 * x * (1.0 + jnp.tanh(c * (x + 0.044715 * (x * x * x))))


def _sigmoid(x):
    return 1.0 / (1.0 + jnp.exp(-x))


def _dot(a, b):
    return jnp.dot(a, b, preferred_element_type=F32)


def _in_proj_kernel(x_ref, xh_ref, g_ref, w_ref, poolw_ref, pscale_ref, lng_ref, lnb_ref, ws_ref, bst_ref,
                    a_ref, qkv1_ref, qkv2_ref, qkv3_ref, ssm_ref, d_ref, slab_ref, ext_ref):
    h = _rms(x_ref[...], g_ref[...]).astype(BF16)
    tm = x_ref.shape[0]
    step = W_BRANCH
    ng = len(DIL_GROUPS)
    nslab = step // V7X_LANES
    i = pl.program_id(1)
    proj = lambda lo: _dot(h, w_ref[:, lo:lo + step])

    lo_sgu = W_BRANCH + N_ATT_COLS + W_BRANCH
    u = _gelu(proj(lo_sgu))
    v = _gelu(proj(lo_sgu + step))
    vc = v - jnp.mean(v, axis=-1, keepdims=True)
    var = jnp.mean(vc * vc, axis=-1, keepdims=True)
    vn = (vc * lax.rsqrt(var + EPS) * lng_ref[...] + lnb_ref[...]).astype(BF16)

    halo = _dot(_rms(xh_ref[...], g_ref[...]).astype(BF16), w_ref[:, 0:step])
    ext_ref[0:POOL_HALO, :] = jnp.where(i == 0, 0.0, halo)
    ext_ref[POOL_HALO:, :] = proj(0)

    def qkv(which):
        dst = slice(which * step, (which + 1) * step)
        for g, out_ref in enumerate((qkv1_ref, qkv2_ref, qkv3_ref)):
            y = proj(W_BRANCH + (which * ng + g) * step)
            if which == 0:
                y = y * ATT_Q_SCALE
            dil = DIL_GROUPS[g][1]
            if dil == 1:
                out_ref[:, dst] = y.astype(BF16)
                continue
            base = ((which * (ng - 1) + g - 1) % IN_PROJ_SLAB_SETS) * nslab
            for c in range(nslab):
                slab_ref[base + c] = y[:, c * V7X_LANES:(c + 1) * V7X_LANES]
            for c in range(nslab):
                cols = slice(which * step + c * V7X_LANES, which * step + (c + 1) * V7X_LANES)
                for r in range(dil):
                    out_ref[r, :, cols] = slab_ref[base + c, pl.ds(r, tm // dil, stride=dil), :].astype(BF16)

    qkv(0)

    tpos = i * tm + lax.broadcasted_iota(jnp.int32, (tm, 1), 0)
    for gi, win in enumerate(POOL_WINDOWS):
        lanes = slice(gi * POOL_GROUP, (gi + 1) * POOL_GROUP)
        tok = ext_ref[POOL_HALO:POOL_HALO + tm, lanes]
        s = tok
        for j in range(1, win):
            s = s + ext_ref[POOL_HALO - j:POOL_HALO - j + tm, lanes]
        cnt = jnp.minimum(tpos + 1, win).astype(F32)
        p = s / cnt - tok
        a_ref[:, lanes] = (_dot(p.astype(BF16), poolw_ref[gi]) * pscale_ref[:, lanes]).astype(BF16)

    qkv(1)

    T = SGU_CHUNK
    gd = W_BRANCH // SGU_GROUPS
    causal = lax.broadcasted_iota(jnp.int32, (T, T), 0) >= lax.broadcasted_iota(jnp.int32, (T, T), 1)
    for g in range(SGU_GROUPS):
        wg = jnp.where(causal, ws_ref[g], 0.0).astype(BF16)
        bias = bst_ref[:, g:g + 1]
        cols = slice(g * gd, (g + 1) * gd)
        for c in range(tm // T):
            rows = slice(c * T, (c + 1) * T)
            sv = _dot(wg, vn[rows, cols]) + bias
            d_ref[rows, cols] = (u[rows, cols] * sv).astype(BF16)

    qkv(2)
    ssm_ref[...] = proj(W_BRANCH + N_ATT_COLS).astype(BF16)


def _residue_major_shape(B, S, dil, width):
    return (B, S // ATT_SPAN, dil, ATT_SPAN // dil, width)


def _in_proj(x, l, g, w, pool_w, pool_scale, ln_g, ln_b, w_s, b_s_t):
    B, S, D = x.shape
    tm = ATT_SPAN
    qw = 3 * W_BRANCH
    d2, d3 = DIL_GROUPS[1][1], DIL_GROUPS[2][1]
    tok = lambda width: pl.BlockSpec((None, tm, width), lambda b, i: (b, i, 0))
    return pl.pallas_call(
        _in_proj_kernel,
        grid=(B, S // tm),
        in_specs=[
            tok(D),
            pl.BlockSpec((None, POOL_HALO, D), lambda b, i: (b, jnp.maximum(i * (tm // POOL_HALO) - 1, 0), 0)),
            _const_spec((1, D)),
            _layer_spec(w.shape, l),
            _const_spec(pool_w.shape), _const_spec(pool_scale.shape), _const_spec(ln_g.shape),
            _const_spec(ln_b.shape), _const_spec(w_s.shape), _const_spec(b_s_t.shape),
        ],
        out_specs=[
            tok(W_BRANCH),
            tok(qw),
            pl.BlockSpec((None, None, d2, tm // d2, qw), lambda b, i: (b, i, 0, 0, 0)),
            pl.BlockSpec((None, None, d3, tm // d3, qw), lambda b, i: (b, i, 0, 0, 0)),
            tok(W_BRANCH),
            tok(W_BRANCH),
        ],
        out_shape=[
            jax.ShapeDtypeStruct((B, S, W_BRANCH), BF16),
            jax.ShapeDtypeStruct((B, S, qw), BF16),
            jax.ShapeDtypeStruct(_residue_major_shape(B, S, d2, qw), BF16),
            jax.ShapeDtypeStruct(_residue_major_shape(B, S, d3, qw), BF16),
            jax.ShapeDtypeStruct((B, S, W_BRANCH), BF16),
            jax.ShapeDtypeStruct((B, S, W_BRANCH), BF16),
        ],
        scratch_shapes=[
            pltpu.VMEM((IN_PROJ_SLAB_SETS * W_BRANCH // V7X_LANES, tm, V7X_LANES), F32),
            pltpu.VMEM((POOL_HALO + tm, W_BRANCH), F32),
        ],
        compiler_params=_params("parallel", "parallel"),
        name="in_proj",
    )(x, x, g, w, pool_w, pool_scale, ln_g, ln_b, w_s, b_s_t)


def _attn_kernel(layout, q_ref, kp_ref, kc_ref, vp_ref, vc_ref, bias_ref, o_ref, st_ref):
    band, W = ATT_BAND, W_BRANCH
    lane = lax.broadcasted_iota(jnp.int32, (1, V7X_LANES), 1)
    low = lane < ATT_HEAD_DIM
    first_step = pl.program_id(1) == 0
    if layout == "natural":
        k_ext = jnp.concatenate([kp_ref[...], kc_ref[...]], axis=0)
        v_ext = jnp.concatenate([vp_ref[...], vc_ref[...]], axis=0)

    def unit(u):
        if layout == "natural":
            rows = slice(u * band, (u + 1) * band)
            table = jnp.where(first_step, 1, 0) if u == 0 else 0
            return (q_ref[rows, :], k_ext[u * band:(u + 2) * band], v_ext[u * band:(u + 2) * band], table,
                    lambda cols, val: o_ref.__setitem__((rows, cols), val),
                    lambda val: st_ref.__setitem__((rows, slice(None)), val))
        table = jnp.where(first_step, 1, 0)
        if layout == "span":
            cat = lambda p, c: jnp.concatenate([p[u], c[u]], axis=0)
            return (q_ref[u], cat(kp_ref, kc_ref), cat(vp_ref, vc_ref), table,
                    lambda cols, val: o_ref.__setitem__((u, slice(None), cols), val),
                    lambda val: st_ref.__setitem__((u,), val))
        sp, per = q_ref.shape[0], q_ref.shape[2]
        get = lambda ref: ref[:, u].reshape(band, W)
        cat = lambda p, c: jnp.concatenate([get(p), get(c)], axis=0)
        return (get(q_ref), cat(kp_ref, kc_ref), cat(vp_ref, vc_ref), table,
                lambda cols, val: o_ref.__setitem__((slice(None), u, slice(None), cols),
                                                    val.reshape(sp, per, V7X_LANES)),
                lambda val: st_ref.__setitem__((slice(None), u), val.reshape(sp, per, V7X_LANES)))

    for u in range(ATT_UNITS):
        q, k, v, table, put_o, put_st = unit(u)
        stats = jnp.zeros((band, V7X_LANES), F32)
        for j in range(ATT_HEADS // 2):
            cols = slice(j * V7X_LANES, (j + 1) * V7X_LANES)
            qp, kp, vp = q[:, cols], k[:, cols], v[:, cols]
            zq, zv = jnp.zeros_like(qp), jnp.zeros_like(vp)
            q2 = jnp.concatenate([jnp.where(low, qp, zq), jnp.where(low, zq, qp)], axis=0)
            s = lax.dot_general(q2, kp, (((1,), (1,)), ((), ())), preferred_element_type=F32)
            s = s + jnp.concatenate([bias_ref[table, 2 * j], bias_ref[table, 2 * j + 1]], axis=0)
            m = jnp.max(s, axis=-1, keepdims=True)
            p = jnp.exp2(s - m)
            l = jnp.sum(p, axis=-1, keepdims=True)
            pb = p.astype(BF16)
            p2 = jnp.concatenate([pb[:band], pb[band:]], axis=1)
            v2 = jnp.concatenate([jnp.where(low, vp, zv), jnp.where(low, zv, vp)], axis=0)
            put_o(cols, _dot(p2, v2).astype(BF16))
            for e in range(2):
                h = 2 * j + e
                stats = jnp.where(lane == h, m[e * band:(e + 1) * band], stats)
                stats = jnp.where(lane == ATT_HEADS + h, l[e * band:(e + 1) * band], stats)
        put_st(stats)


def _attn_group(qkv, bias, dil):
    B = qkv.shape[0]
    W = W_BRANCH
    U = ATT_UNITS
    if dil == 1:
        S = qkv.shape[1]
        layout, grid = "natural", (B, S // (U * ATT_BAND))
        blk = lambda width: (None, U * ATT_BAND, width)
        prev_blk = (None, ATT_BAND, W)
        idx = lambda which: (lambda b, i: (b, i, which))
        idx_prev = lambda which: (lambda b, i: (b, jnp.maximum(U * i - 1, 0), which))
        out_shape = lambda width: (B, S, width)
        sem = ("parallel", "parallel")
    else:
        nspan, _, per, _ = qkv.shape[1:]
        spans = ATT_BAND // per
        if spans == 1:
            assert dil == U
            layout, grid = "span", (B, nspan)
            blk = lambda width: (None, None, U, per, width)
            idx = lambda which: (lambda b, n: (b, n, 0, 0, which))
            idx_prev = lambda which: (lambda b, n: (b, jnp.maximum(n - 1, 0), 0, 0, which))
            sem = ("parallel", "parallel")
        else:
            layout, grid = "band", (B, nspan // spans, dil // U)
            blk = lambda width: (None, spans, U, per, width)
            idx = lambda which: (lambda b, n, r: (b, n, r, 0, which))
            idx_prev = lambda which: (lambda b, n, r: (b, jnp.maximum(n - 1, 0), r, 0, which))
            sem = ("parallel", "parallel", "parallel")
        prev_blk = blk(W)
        out_shape = lambda width: (B, nspan, dil, per, width)
    return pl.pallas_call(
        functools.partial(_attn_kernel, layout),
        grid=grid,
        in_specs=[
            pl.BlockSpec(blk(W), idx(0)),
            pl.BlockSpec(prev_blk, idx_prev(1)),
            pl.BlockSpec(blk(W), idx(1)),
            pl.BlockSpec(prev_blk, idx_prev(2)),
            pl.BlockSpec(blk(W), idx(2)),
            _const_spec(bias.shape),
        ],
        out_specs=[
            pl.BlockSpec(blk(W), idx(0)),
            pl.BlockSpec(blk(V7X_LANES), idx(0)),
        ],
        out_shape=[
            jax.ShapeDtypeStruct(out_shape(W), BF16),
            jax.ShapeDtypeStruct(out_shape(V7X_LANES), F32),
        ],
        compiler_params=_params(*sem),
        name=f"attn_d{dil}",
    )(qkv, qkv, qkv, qkv, qkv, bias)


def _attn_mix(o1_ref, o2_ref, o3_ref, s1_ref, s2_ref, s3_ref, out_ref, on_ref, sn_ref):
    lane = lax.broadcasted_iota(jnp.int32, (1, V7X_LANES), 1)
    low = lane < ATT_HEAD_DIM
    nslab = W_BRANCH // V7X_LANES
    for gi, (o_ref, s_ref) in enumerate(((o2_ref, s2_ref), (o3_ref, s3_ref))):
        dil, per = o_ref.shape[0], o_ref.shape[1]
        for r in range(dil):
            sn_ref[gi, pl.ds(r, per, stride=dil), :] = s_ref[r]
            for c in range(nslab):
                on_ref[gi * nslab + c, pl.ds(r, per, stride=dil), :] = (
                    o_ref[r, :, c * V7X_LANES:(c + 1) * V7X_LANES].astype(F32))
    stats = (s1_ref[...], sn_ref[0], sn_ref[1])
    for j in range(ATT_HEADS // 2):
        cols = slice(j * V7X_LANES, (j + 1) * V7X_LANES)
        wts = []
        for e in range(2):
            h = 2 * j + e
            ms = [x[:, h:h + 1] for x in stats]
            ls = [x[:, ATT_HEADS + h:ATT_HEADS + h + 1] for x in stats]
            top = jnp.maximum(jnp.maximum(ms[0], ms[1]), ms[2])
            ex = [jnp.exp2(x - top) for x in ms]
            inv = 1.0 / (ex[0] * ls[0] + ex[1] * ls[1] + ex[2] * ls[2])
            wts.append([x * inv for x in ex])
        vals = (o1_ref[:, cols].astype(F32), on_ref[j], on_ref[nslab + j])
        acc = None
        for gi in range(3):
            term = jnp.where(low, wts[0][gi], wts[1][gi]) * vals[gi]
            acc = term if acc is None else acc + term
        out_ref[:, cols] = acc.astype(BF16)


def _s5_kernel(u_ref, wbr_ref, wbi_ref, ar_ref, ai_ref, cr_ref, ci_ref, dsk_ref, wglu_ref, bglu_ref,
               out_ref, h_ref, slab_ref, g_ref):
    nst = SSM_STATES
    nb, tc, W = u_ref.shape
    pitch = S5_PITCH
    lanes = V7X_LANES
    nslab = nst // lanes

    @pl.when(pl.program_id(0) == 0)
    def _():
        h_ref[...] = jnp.zeros_like(h_ref)

    u = u_ref[...].reshape(nb * tc, W)

    tile = V7X_MXU_DIM
    chan_tile_states = tile * SSM_STATE // SSM_GROUP
    for j in range(nst // tile):
        kt = (j * tile // chan_tile_states) * tile
        uk = u[:, kt:kt + tile]
        for part, w_ref in enumerate((wbr_ref, wbi_ref)):
            y = _dot(uk, w_ref[kt:kt + tile, j * tile:(j + 1) * tile])
            for half in range(tile // lanes):
                sl = part * nslab + j * (tile // lanes) + half
                for b in range(nb):
                    slab_ref[sl, b * pitch:b * pitch + tc, :] = y[b * tc:(b + 1) * tc, half * lanes:(half + 1) * lanes]

    per = S5_LANE_CHUNK // lanes
    for c in range(nslab // per):
        sl_r = [c * per + s for s in range(per)]
        sl_i = [nslab + c * per + s for s in range(per)]
        ar = [jnp.broadcast_to(ar_ref[:, s * lanes:(s + 1) * lanes], (nb, lanes)) for s in sl_r]
        ai = [jnp.broadcast_to(ai_ref[:, s * lanes:(s + 1) * lanes], (nb, lanes)) for s in sl_r]

        def body(t, carry, sl_r=sl_r, sl_i=sl_i, ar=ar, ai=ai):
            hr, hi = carry
            nr, ni = [], []
            for s in range(per):
                rows = pl.ds(t, nb, stride=pitch)
                br = slab_ref[sl_r[s], rows, :]
                bi = slab_ref[sl_i[s], rows, :]
                r_new = ar[s] * hr[s] - ai[s] * hi[s] + br
                i_new = ar[s] * hi[s] + ai[s] * hr[s] + bi
                slab_ref[sl_r[s], rows, :] = r_new
                slab_ref[sl_i[s], rows, :] = i_new
                nr.append(r_new)
                ni.append(i_new)
            return tuple(nr), tuple(ni)

        init = (tuple(h_ref[:, s * lanes:(s + 1) * lanes] for s in sl_r),
                tuple(h_ref[:, s * lanes:(s + 1) * lanes] for s in sl_i))
        hr, hi = lax.fori_loop(0, tc, body, init, unroll=4)
        for k, s in enumerate(sl_r):
            h_ref[:, s * lanes:(s + 1) * lanes] = hr[k]
        for k, s in enumerate(sl_i):
            h_ref[:, s * lanes:(s + 1) * lanes] = hi[k]

    def states(first_slab, count):
        return jnp.concatenate(
            [jnp.concatenate([slab_ref[first_slab + s, b * pitch:b * pitch + tc, :] for b in range(nb)], axis=0)
             for s in range(count)], axis=1).astype(BF16)

    cnt = chan_tile_states // lanes
    for n in range(W_BRANCH // tile):
        s0 = n * chan_tile_states
        cols = slice(n * tile, (n + 1) * tile)
        yr = _dot(states(n * cnt, cnt), cr_ref[s0:s0 + chan_tile_states, cols])
        yi = _dot(states(nslab + n * cnt, cnt), ci_ref[s0:s0 + chan_tile_states, cols])
        y = yr - yi + u[:, cols].astype(F32) * dsk_ref[:, cols]
        g_ref[:, cols] = _gelu(y)
    g = g_ref[...]
    z = _dot(g.astype(BF16), wglu_ref[...]) + bglu_ref[...]
    out_ref[...] = (g * _sigmoid(z)).astype(BF16).reshape(nb, tc, W)


def _s5(u, l, wbr, wbi, ar, ai, cr, ci, dskip, wglu, bglu):
    B, S, W = u.shape
    tc = S5_STEPS
    blk = pl.BlockSpec((B, tc, W), lambda k: (0, k, 0))
    return pl.pallas_call(
        _s5_kernel,
        grid=(S // tc,),
        in_specs=[blk] + [_const_spec(a.shape) for a in (wbr, wbi, ar, ai, cr, ci, dskip)]
        + [_layer_spec(wglu.shape, l), _const_spec(bglu.shape)],
        out_specs=blk,
        out_shape=jax.ShapeDtypeStruct((B, S, W), BF16),
        scratch_shapes=[
            pltpu.VMEM((B, 2 * SSM_STATES), F32),
            pltpu.VMEM((2 * SSM_STATES // V7X_LANES, B * S5_PITCH, V7X_LANES), F32),
            pltpu.VMEM((B * tc, W), F32),
        ],
        compiler_params=_params("arbitrary"),
        name="s5",
    )(u, wbr, wbi, ar, ai, cr, ci, dskip, wglu, bglu)


def _s5_params(a_re, a_im, log_dt, b_re, b_im, c_re, c_im):
    G, P, C = SSM_GROUPS, SSM_STATE, SSM_GROUP
    lam_re = jnp.minimum(a_re, -1e-4)
    lam_im = a_im
    dt = jnp.exp(log_dt)[:, None]
    mag = jnp.exp(lam_re * dt)
    ab_re, ab_im = mag * jnp.cos(lam_im * dt), mag * jnp.sin(lam_im * dt)
    den = lam_re * lam_re + lam_im * lam_im
    f_re = ((ab_re - 1.0) * lam_re + ab_im * lam_im) / den
    f_im = (ab_im * lam_re - (ab_re - 1.0) * lam_im) / den
    bb_re = f_re[..., None] * b_re - f_im[..., None] * b_im
    bb_im = f_re[..., None] * b_im + f_im[..., None] * b_re
    eye = jnp.eye(G, dtype=F32)

    def in_mat(bb):
        return jnp.einsum('gpc,gh->gchp', bb, eye).reshape(G * C, G * P).astype(BF16)

    def out_mat(cc):
        return jnp.einsum('gcp,gh->gphc', cc, eye).reshape(G * P, G * C).astype(BF16)

    return (in_mat(bb_re), in_mat(bb_im), ab_re.reshape(1, G * P), ab_im.reshape(1, G * P),
            out_mat(c_re), out_mat(c_im))


def _merge_kernel(x_ref, a_ref, o1_ref, o2_ref, o3_ref, s1_ref, s2_ref, s3_ref, c_ref, d_ref,
                  gpre_ref, wg_ref, gb_ref, wup_ref, wout_ref, gpost_ref, out_ref, b_ref, on_ref, sn_ref):
    _attn_mix(o1_ref, o2_ref, o3_ref, s1_ref, s2_ref, s3_ref, b_ref, on_ref, sn_ref)
    x = x_ref[...]
    h = _rms(x, gpre_ref[...]).astype(BF16)
    merged = None
    for i, br in enumerate((a_ref, b_ref, c_ref, d_ref)):
        gate = _sigmoid(_dot(h, wg_ref[:, i * D_MODEL:(i + 1) * D_MODEL]) + gb_ref[i:i + 1, :])
        term = gate * _dot(br[...], wup_ref[i])
        merged = term if merged is None else merged + term
    y = _dot(merged.astype(BF16), wout_ref[...])
    out_ref[...] = x + _rms(y, gpost_ref[...])


def _merge(x, l, a, outs, stats, c, d, gpre, wg, gb, wup, wout, gpost):
    B, S, D = x.shape
    tm = ATT_SPAN
    W = W_BRANCH

    def tok(a_):
        if a_.ndim == 3:
            return pl.BlockSpec((None, tm, a_.shape[-1]), lambda b_, i: (b_, i, 0))
        return pl.BlockSpec((None, None) + a_.shape[2:], lambda b_, i: (b_, i, 0, 0, 0))

    return pl.pallas_call(
        _merge_kernel,
        grid=(B, S // tm),
        in_specs=[tok(a_) for a_ in (x, a, *outs, *stats, c, d)] + [
            _const_spec((1, D)), _layer_spec(wg.shape, l), _const_spec(gb.shape), _layer_spec(wup.shape, l),
            _layer_spec(wout.shape, l), _const_spec((1, D)),
        ],
        out_specs=tok(x),
        out_shape=jax.ShapeDtypeStruct((B, S, D), F32),
        scratch_shapes=[
            pltpu.VMEM((tm, W), BF16),
            pltpu.VMEM((2 * W // V7X_LANES, tm, V7X_LANES), F32),
            pltpu.VMEM((2, tm, V7X_LANES), F32),
        ],
        compiler_params=_params("parallel", "parallel"),
        name="merge",
    )(x, a, *outs, *stats, c, d, gpre, wg, gb, wup, wout, gpost)


def _mem_kv_kernel(mem_ref, g_ref, w_ref, k_ref, v_ref):
    mn = _rms(mem_ref[...], g_ref[...]).astype(BF16)
    hw = X_HEADS * X_HEAD_DIM
    k_ref[...] = _dot(mn, w_ref[:, :hw]).astype(BF16)
    v_ref[...] = _dot(mn, w_ref[:, hw:]).astype(BF16)


def _mem_kv(mem, l, g, w):
    B, M, D = mem.shape
    hw = X_HEADS * X_HEAD_DIM
    ob = pl.BlockSpec((None, M, hw), lambda b: (b, 0, 0))
    return pl.pallas_call(
        _mem_kv_kernel,
        grid=(B,),
        in_specs=[pl.BlockSpec((None, M, D), lambda b: (b, 0, 0)), _const_spec((1, D)), _layer_spec(w.shape, l)],
        out_specs=[ob, ob],
        out_shape=[jax.ShapeDtypeStruct((B, M, hw), BF16)] * 2,
        compiler_params=_params("parallel"),
        name="mem_kv",
    )(mem, g, w)


def _xattn_ffn_kernel(x_ref, k_ref, v_ref, gxpre_ref, wq_ref, wo_ref, gxpost_ref, gfpre_ref, w1_ref, w2_ref,
                      gfpost_ref, out_ref, o_ref, acc_ref):
    x = x_ref[...]
    h = _rms(x, gxpre_ref[...]).astype(BF16)
    q = (_dot(h, wq_ref[...]) * (X_HEAD_DIM ** -0.5)).astype(BF16)
    for hd in range(X_HEADS):
        cols = slice(hd * X_HEAD_DIM, (hd + 1) * X_HEAD_DIM)
        s = lax.dot_general(q[:, cols], k_ref[:, cols], (((1,), (1,)), ((), ())), preferred_element_type=F32)
        m = jnp.max(s, axis=-1, keepdims=True)
        p = jnp.exp(s - m)
        l = jnp.sum(p, axis=-1, keepdims=True)
        o_ref[:, cols] = (_dot(p.astype(BF16), v_ref[:, cols]) * (1.0 / l)).astype(BF16)
    x = x + _rms(_dot(o_ref[...], wo_ref[...]), gxpost_ref[...])
    h = _rms(x, gfpre_ref[...]).astype(BF16)
    for c in range(D_FF // FF_CHUNK):
        cols = slice(c * FF_CHUNK, (c + 1) * FF_CHUNK)
        a = jnp.maximum(_dot(h, w1_ref[:, cols]), 0.0)
        a = (a * a).astype(BF16)
        part = _dot(a, w2_ref[cols, :])
        if c == 0:
            acc_ref[...] = part
        else:
            acc_ref[...] += part
    out_ref[...] = x + _rms(acc_ref[...], gfpost_ref[...])


def _xattn_ffn(x, l, k, v, gxpre, wq, wo, gxpost, gfpre, w1, w2, gfpost):
    B, S, D = x.shape
    tm = TOKEN_BLOCK
    hw = X_HEADS * X_HEAD_DIM
    xb = pl.BlockSpec((None, tm, D), lambda b, i: (b, i, 0))
    kb = pl.BlockSpec((None, N_MEM, hw), lambda b, i: (b, 0, 0))
    consts = (gxpre, wq, wo, gxpost, gfpre, w1, w2, gfpost)
    return pl.pallas_call(
        _xattn_ffn_kernel,
        grid=(B, S // tm),
        in_specs=[xb, kb, kb] + [_layer_spec(a.shape, l) if a.ndim == 3 else _const_spec(a.shape) for a in consts],
        out_specs=xb,
        out_shape=jax.ShapeDtypeStruct((B, S, D), F32),
        scratch_shapes=[pltpu.VMEM((tm, hw), BF16), pltpu.VMEM((tm, D), F32)],
        compiler_params=_params("parallel", "parallel"),
        name="xattn_ffn",
    )(x, k, v, *consts)


def _t5_bucket(n):
    exact = REL_BUCKETS // 2
    nf = np.maximum(n, 1).astype(np.float32)
    large = exact + (np.log(nf / exact) / np.log(REL_MAX_DIST / exact) * (REL_BUCKETS - exact)).astype(np.int32)
    large = np.minimum(large, REL_BUCKETS - 1)
    return np.where(n < exact, n, large).astype(np.int32)


def _band_bias(rel_bias, g, band, dil):
    d = np.arange(band + 1)
    table = rel_bias[jnp.asarray(_t5_bucket(d * dil))][:, g * ATT_HEADS:(g + 1) * ATT_HEADS].astype(F32) * LOG2E
    pad = jnp.full((ATT_HEADS, band - 1), NEG_INF, F32)
    f = jnp.concatenate([pad, table[::-1].T, pad], axis=1)
    full = jnp.stack([f[:, band - 1 - i:3 * band - 1 - i] for i in range(band)], axis=1)
    no_prev = jnp.concatenate([jnp.full((ATT_HEADS, band, band), NEG_INF, F32), full[:, :, band:]], axis=2)
    return jnp.stack([full, no_prev], axis=0)


def kernel(x, mem, rel_bias, g_mix_pre, g_mix_post, w_in, gate_b, pool_w, pool_scale, a_re, a_im, log_dt,
           b_re, b_im, c_re, c_im, d_skip, w_glu, b_glu, sgu_ln_g, sgu_ln_b, w_s, b_s, w_up, w_out,
           g_x_pre, g_x_post, g_mem, w_cq, w_ckv, w_co, g_ff_pre, g_ff_post, w_ff1, w_ff2):
    B, S, D = x.shape
    depth = w_in.shape[0]
    assert D == D_MODEL and S % (TOKEN_BLOCK) == 0 and B == V7X_SUBLANES
    for win, dil in DIL_GROUPS:
        assert win // dil == ATT_BAND and (S // dil) % ATT_BAND == 0

    biases = [_band_bias(rel_bias, g, win // dil, dil) for g, (win, dil) in enumerate(DIL_GROUPS)]
    row = lambda a: a.reshape(1, -1).astype(F32)
    w_mix, w_gate = w_in[:, :, :OFF_GATE].astype(BF16), w_in[:, :, OFF_GATE:].astype(BF16)
    w_up, w_out, w_glu, w_cq, w_ckv, w_co, w_ff1, w_ff2 = (
        a.astype(BF16) for a in (w_up, w_out, w_glu, w_cq, w_ckv, w_co, w_ff1, w_ff2))

    for l in range(depth):
        a_out, qkv1, qkv2, qkv3, ssm_in, d_out = _in_proj(
            x, l, row(g_mix_pre[l]), w_mix, pool_w[l].astype(BF16), row(pool_scale[l]),
            row(sgu_ln_g[l]), row(sgu_ln_b[l]), w_s[l], b_s[l].T)
        outs, stats = [], []
        for qkv, bias, (win, dil) in zip((qkv1, qkv2, qkv3), biases, DIL_GROUPS):
            o, st = _attn_group(qkv, bias, dil)
            outs.append(o)
            stats.append(st)
        s5p = _s5_params(a_re[l], a_im[l], log_dt[l], b_re[l], b_im[l], c_re[l], c_im[l])
        c_out = _s5(ssm_in, l, *s5p, row(d_skip[l]), w_glu, row(b_glu[l]))
        x = _merge(x, l, a_out, outs, stats, c_out, d_out, row(g_mix_pre[l]), w_gate, gate_b[l], w_up, w_out,
                   row(g_mix_post[l]))
        k_mem, v_mem = _mem_kv(mem, l, row(g_mem[l]), w_ckv)
        x = _xattn_ffn(x, l, k_mem, v_mem, row(g_x_pre[l]), w_cq, w_co, row(g_x_post[l]),
                       row(g_ff_pre[l]), w_ff1, w_ff2, row(g_ff_post[l]))
    return x
```

```python
import functools
import math

import jax
import jax.numpy as jnp
import numpy as np
from jax import lax
from jax.experimental import pallas as pl
from jax.experimental.pallas import tpu as pltpu

F32 = jnp.float32
BF16 = jnp.bfloat16

D_MODEL = 1024
N_MEM = 256
N_BRANCH = 4
W_BRANCH = D_MODEL // 2
POOL_WINDOWS = (2, 4, 8, 16)
POOL_GROUP = W_BRANCH // len(POOL_WINDOWS)
DIL_GROUPS = ((128, 1), (512, 4), (2048, 16))
ATT_HEADS = 8
ATT_HEAD_DIM = W_BRANCH // ATT_HEADS
SSM_GROUP = 16
SSM_GROUPS = W_BRANCH // SSM_GROUP
SSM_STATE = 64
SSM_STATES = SSM_GROUPS * SSM_STATE
SGU_CHUNK = 128
SGU_GROUPS = 4
X_HEADS = 4
X_HEAD_DIM = 128
D_FF = 4 * D_MODEL
REL_BUCKETS = 32
REL_MAX_DIST = 2048
EPS = 1e-6
NEG_INF = -1e30
N_ATT_COLS = 3 * len(DIL_GROUPS) * W_BRANCH
OFF_GATE = W_BRANCH + N_ATT_COLS + W_BRANCH + 2 * W_BRANCH

V7X_LANES = 128
V7X_SUBLANES = 8
V7X_MXU_DIM = 256
V7X_VMEM_LIMIT_BYTES = 56 * 1024 * 1024

TOKEN_BLOCK = 512
POOL_HALO = 16
ATT_BAND = 128
ATT_SPAN = 512
ATT_UNITS = 4
IN_PROJ_SLAB_SETS = 3
LOG2E = 1.4426950408889634
ATT_Q_SCALE = ATT_HEAD_DIM ** -0.5 * LOG2E
S5_STEPS = 64
S5_PITCH = S5_STEPS + 8
S5_LANE_CHUNK = 512
FF_CHUNK = 512


def _params(*sem):
    return pltpu.CompilerParams(dimension_semantics=sem, vmem_limit_bytes=V7X_VMEM_LIMIT_BYTES)


def _const_spec(shape):
    nd = len(shape)
    return pl.BlockSpec(shape, lambda *_: (0,) * nd, pipeline_mode=pl.Buffered(1))


def _layer_spec(shape, l):
    nd = len(shape) - 1
    return pl.BlockSpec((None,) + tuple(shape[1:]), lambda *_: (l,) + (0,) * nd, pipeline_mode=pl.Buffered(1))


def _rms(xf, g):
    return xf * lax.rsqrt(jnp.mean(xf * xf, axis=-1, keepdims=True) + EPS) * g


def _gelu(x):
    c = math.sqrt(2.0 / math.pi)
    return 0.5 * x * (1.0 + jnp.tanh(c * (x + 0.044715 * (x * x * x))))


def _sigmoid(x):
    return 1.0 / (1.0 + jnp.exp(-x))


def _dot(a, b):
    return jnp.dot(a, b, preferred_element_type=F32)


def _in_proj_kernel(x_ref, xh_ref, g_ref, w_ref, poolw_ref, pscale_ref, lng_ref, lnb_ref, ws_ref, bst_ref,
                    a_ref, qkv1_ref, qkv2_ref, qkv3_ref, ssm_ref, d_ref, slab_ref, ext_ref):
    h = _rms(x_ref[...], g_ref[...]).astype(BF16)
    tm = x_ref.shape[0]
    step = W_BRANCH
    ng = len(DIL_GROUPS)
    nslab = step // V7X_LANES
    i = pl.program_id(1)
    proj = lambda lo: _dot(h, w_ref[:, lo:lo + step])

    lo_sgu = W_BRANCH + N_ATT_COLS + W_BRANCH
    u = _gelu(proj(lo_sgu))
    v = _gelu(proj(lo_sgu + step))
    vc = v - jnp.mean(v, axis=-1, keepdims=True)
    var = jnp.mean(vc * vc, axis=-1, keepdims=True)
    vn = (vc * lax.rsqrt(var + EPS) * lng_ref[...] + lnb_ref[...]).astype(BF16)

    halo = _dot(_rms(xh_ref[...], g_ref[...]).astype(BF16), w_ref[:, 0:step])
    ext_ref[0:POOL_HALO, :] = jnp.where(i == 0, 0.0, halo)
    ext_ref[POOL_HALO:, :] = proj(0)

    def qkv(which):
        dst = slice(which * step, (which + 1) * step)
        for g, out_ref in enumerate((qkv1_ref, qkv2_ref, qkv3_ref)):
            y = proj(W_BRANCH + (which * ng + g) * step)
            if which == 0:
                y = y * ATT_Q_SCALE
            dil = DIL_GROUPS[g][1]
            if dil == 1:
                out_ref[:, dst] = y.astype(BF16)
                continue
            base = ((which * (ng - 1) + g - 1) % IN_PROJ_SLAB_SETS) * nslab
            for c in range(nslab):
                slab_ref[base + c] = y[:, c * V7X_LANES:(c + 1) * V7X_LANES]
            for c in range(nslab):
                cols = slice(which * step + c * V7X_LANES, which * step + (c + 1) * V7X_LANES)
                for r in range(dil):
                    out_ref[r, :, cols] = slab_ref[base + c, pl.ds(r, tm // dil, stride=dil), :].astype(BF16)

    qkv(0)

    tpos = i * tm + lax.broadcasted_iota(jnp.int32, (tm, 1), 0)
    for gi, win in enumerate(POOL_WINDOWS):
        lanes = slice(gi * POOL_GROUP, (gi + 1) * POOL_GROUP)
        tok = ext_ref[POOL_HALO:POOL_HALO + tm, lanes]
        s = tok
        for j in range(1, win):
            s = s + ext_ref[POOL_HALO - j:POOL_HALO - j + tm, lanes]
        cnt = jnp.minimum(tpos + 1, win).astype(F32)
        p = s / cnt - tok
        a_ref[:, lanes] = (_dot(p.astype(BF16), poolw_ref[gi]) * pscale_ref[:, lanes]).astype(BF16)

    qkv(1)

    T = SGU_CHUNK
    gd = W_BRANCH // SGU_GROUPS
    causal = lax.broadcasted_iota(jnp.int32, (T, T), 0) >= lax.broadcasted_iota(jnp.int32, (T, T), 1)
    for g in range(SGU_GROUPS):
        wg = jnp.where(causal, ws_ref[g], 0.0).astype(BF16)
        bias = bst_ref[:, g:g + 1]
        cols = slice(g * gd, (g + 1) * gd)
        for c in range(tm // T):
            rows = slice(c * T, (c + 1) * T)
            sv = _dot(wg, vn[rows, cols]) + bias
            d_ref[rows, cols] = (u[rows, cols] * sv).astype(BF16)

    qkv(2)
    ssm_ref[...] = proj(W_BRANCH + N_ATT_COLS).astype(BF16)


def _residue_major_shape(B, S, dil, width):
    return (B, S // ATT_SPAN, dil, ATT_SPAN // dil, width)


def _in_proj(x, l, g, w, pool_w, pool_scale, ln_g, ln_b, w_s, b_s_t):
    B, S, D = x.shape
    tm = ATT_SPAN
    qw = 3 * W_BRANCH
    d2, d3 = DIL_GROUPS[1][1], DIL_GROUPS[2][1]
    tok = lambda width: pl.BlockSpec((None, tm, width), lambda b, i: (b, i, 0))
    return pl.pallas_call(
        _in_proj_kernel,
        grid=(B, S // tm),
        in_specs=[
            tok(D),
            pl.BlockSpec((None, POOL_HALO, D), lambda b, i: (b, jnp.maximum(i * (tm // POOL_HALO) - 1, 0), 0)),
            _const_spec((1, D)),
            _layer_spec(w.shape, l),
            _const_spec(pool_w.shape), _const_spec(pool_scale.shape), _const_spec(ln_g.shape),
            _const_spec(ln_b.shape), _const_spec(w_s.shape), _const_spec(b_s_t.shape),
        ],
        out_specs=[
            tok(W_BRANCH),
            tok(qw),
            pl.BlockSpec((None, None, d2, tm // d2, qw), lambda b, i: (b, i, 0, 0, 0)),
            pl.BlockSpec((None, None, d3, tm // d3, qw), lambda b, i: (b, i, 0, 0, 0)),
            tok(W_BRANCH),
            tok(W_BRANCH),
        ],
        out_shape=[
            jax.ShapeDtypeStruct((B, S, W_BRANCH), BF16),
            jax.ShapeDtypeStruct((B, S, qw), BF16),
            jax.ShapeDtypeStruct(_residue_major_shape(B, S, d2, qw), BF16),
            jax.ShapeDtypeStruct(_residue_major_shape(B, S, d3, qw), BF16),
            jax.ShapeDtypeStruct((B, S, W_BRANCH), BF16),
            jax.ShapeDtypeStruct((B, S, W_BRANCH), BF16),
        ],
        scratch_shapes=[
            pltpu.VMEM((IN_PROJ_SLAB_SETS * W_BRANCH // V7X_LANES, tm, V7X_LANES), F32),
            pltpu.VMEM((POOL_HALO + tm, W_BRANCH), F32),
        ],
        compiler_params=_params("parallel", "parallel"),
        name="in_proj",
    )(x, x, g, w, pool_w, pool_scale, ln_g, ln_b, w_s, b_s_t)


def _attn_kernel(layout, q_ref, kp_ref, kc_ref, vp_ref, vc_ref, bias_ref, o_ref, st_ref):
    band, W = ATT_BAND, W_BRANCH
    lane = lax.broadcasted_iota(jnp.int32, (1, V7X_LANES), 1)
    low = lane < ATT_HEAD_DIM
    first_step = pl.program_id(1) == 0
    if layout == "natural":
        k_ext = jnp.concatenate([kp_ref[...], kc_ref[...]], axis=0)
        v_ext = jnp.concatenate([vp_ref[...], vc_ref[...]], axis=0)

    def unit(u):
        if layout == "natural":
            rows = slice(u * band, (u + 1) * band)
            table = jnp.where(first_step, 1, 0) if u == 0 else 0
            return (q_ref[rows, :], k_ext[u * band:(u + 2) * band], v_ext[u * band:(u + 2) * band], table,
                    lambda cols, val: o_ref.__setitem__((rows, cols), val),
                    lambda val: st_ref.__setitem__((rows, slice(None)), val))
        table = jnp.where(first_step, 1, 0)
        if layout == "span":
            cat = lambda p, c: jnp.concatenate([p[u], c[u]], axis=0)
            return (q_ref[u], cat(kp_ref, kc_ref), cat(vp_ref, vc_ref), table,
                    lambda cols, val: o_ref.__setitem__((u, slice(None), cols), val),
                    lambda val: st_ref.__setitem__((u,), val))
        sp, per = q_ref.shape[0], q_ref.shape[2]
        get = lambda ref: ref[:, u].reshape(band, W)
        cat = lambda p, c: jnp.concatenate([get(p), get(c)], axis=0)
        return (get(q_ref), cat(kp_ref, kc_ref), cat(vp_ref, vc_ref), table,
                lambda cols, val: o_ref.__setitem__((slice(None), u, slice(None), cols),
                                                    val.reshape(sp, per, V7X_LANES)),
                lambda val: st_ref.__setitem__((slice(None), u), val.reshape(sp, per, V7X_LANES)))

    def score(q, k, table, j):
        cols = slice(j * V7X_LANES, (j + 1) * V7X_LANES)
        qp = q[:, cols]
        zq = jnp.zeros_like(qp)
        q2 = jnp.concatenate([jnp.where(low, qp, zq), jnp.where(low, zq, qp)], axis=0)
        s = lax.dot_general(q2, k[:, cols], (((1,), (1,)), ((), ())), preferred_element_type=F32)
        return s + jnp.concatenate([bias_ref[table, 2 * j], bias_ref[table, 2 * j + 1]], axis=0)

    tasks = [(u, j) for u in range(ATT_UNITS) for j in range(ATT_HEADS // 2)]
    units = {}
    s_next = None
    for idx, (u, j) in enumerate(tasks):
        if u not in units:
            units[u] = unit(u)
        if j == 0:
            stats = jnp.zeros((band, V7X_LANES), F32)
        q, k, v, table, put_o, put_st = units[u]
        s = score(q, k, table, j) if s_next is None else s_next
        if idx + 1 < len(tasks):
            un, jn = tasks[idx + 1]
            if un not in units:
                units[un] = unit(un)
            s_next = score(units[un][0], units[un][1], units[un][3], jn)
        cols = slice(j * V7X_LANES, (j + 1) * V7X_LANES)
        m = jnp.max(s, axis=-1, keepdims=True)
        p = jnp.exp2(s - m)
        l = jnp.sum(p, axis=-1, keepdims=True)
        pb = p.astype(BF16)
        p2 = jnp.concatenate([pb[:band], pb[band:]], axis=1)
        vp = v[:, cols]
        zv = jnp.zeros_like(vp)
        v2 = jnp.concatenate([jnp.where(low, vp, zv), jnp.where(low, zv, vp)], axis=0)
        put_o(cols, _dot(p2, v2).astype(BF16))
        for e in range(2):
            h = 2 * j + e
            stats = jnp.where(lane == h, m[e * band:(e + 1) * band], stats)
            stats = jnp.where(lane == ATT_HEADS + h, l[e * band:(e + 1) * band], stats)
        if j == ATT_HEADS // 2 - 1:
            put_st(stats)


def _attn_group(qkv, bias, dil):
    B = qkv.shape[0]
    W = W_BRANCH
    U = ATT_UNITS
    if dil == 1:
        S = qkv.shape[1]
        layout, grid = "natural", (B, S // (U * ATT_BAND))
        blk = lambda width: (None, U * ATT_BAND, width)
        prev_blk = (None, ATT_BAND, W)
        idx = lambda which: (lambda b, i: (b, i, which))
        idx_prev = lambda which: (lambda b, i: (b, jnp.maximum(U * i - 1, 0), which))
        out_shape = lambda width: (B, S, width)
        sem = ("parallel", "parallel")
    else:
        nspan, _, per, _ = qkv.shape[1:]
        spans = ATT_BAND // per
        if spans == 1:
            assert dil == U
            layout, grid = "span", (B, nspan)
            blk = lambda width: (None, None, U, per, width)
            idx = lambda which: (lambda b, n: (b, n, 0, 0, which))
            idx_prev = lambda which: (lambda b, n: (b, jnp.maximum(n - 1, 0), 0, 0, which))
            sem = ("parallel", "parallel")
        else:
            layout, grid = "band", (B, nspan // spans, dil // U)
            blk = lambda width: (None, spans, U, per, width)
            idx = lambda which: (lambda b, n, r: (b, n, r, 0, which))
            idx_prev = lambda which: (lambda b, n, r: (b, jnp.maximum(n - 1, 0), r, 0, which))
            sem = ("parallel", "parallel", "parallel")
        prev_blk = blk(W)
        out_shape = lambda width: (B, nspan, dil, per, width)
    return pl.pallas_call(
        functools.partial(_attn_kernel, layout),
        grid=grid,
        in_specs=[
            pl.BlockSpec(blk(W), idx(0)),
            pl.BlockSpec(prev_blk, idx_prev(1)),
            pl.BlockSpec(blk(W), idx(1)),
            pl.BlockSpec(prev_blk, idx_prev(2)),
            pl.BlockSpec(blk(W), idx(2)),
            _const_spec(bias.shape),
        ],
        out_specs=[
            pl.BlockSpec(blk(W), idx(0)),
            pl.BlockSpec(blk(V7X_LANES), idx(0)),
        ],
        out_shape=[
            jax.ShapeDtypeStruct(out_shape(W), BF16),
            jax.ShapeDtypeStruct(out_shape(V7X_LANES), F32),
        ],
        compiler_params=_params(*sem),
        name=f"attn_d{dil}",
    )(qkv, qkv, qkv, qkv, qkv, bias)


def _attn_mix(o1_ref, o2_ref, o3_ref, s1_ref, s2_ref, s3_ref, out_ref, on_ref, sn_ref):
    lane = lax.broadcasted_iota(jnp.int32, (1, V7X_LANES), 1)
    low = lane < ATT_HEAD_DIM
    nslab = W_BRANCH // V7X_LANES
    for gi, (o_ref, s_ref) in enumerate(((o2_ref, s2_ref), (o3_ref, s3_ref))):
        dil, per = o_ref.shape[0], o_ref.shape[1]
        for r in range(dil):
            sn_ref[gi, pl.ds(r, per, stride=dil), :] = s_ref[r]
            for c in range(nslab):
                on_ref[gi * nslab + c, pl.ds(r, per, stride=dil), :] = (
                    o_ref[r, :, c * V7X_LANES:(c + 1) * V7X_LANES].astype(F32))
    stats = (s1_ref[...], sn_ref[0], sn_ref[1])
    for j in range(ATT_HEADS // 2):
        cols = slice(j * V7X_LANES, (j + 1) * V7X_LANES)
        wts = []
        for e in range(2):
            h = 2 * j + e
            ms = [x[:, h:h + 1] for x in stats]
            ls = [x[:, ATT_HEADS + h:ATT_HEADS + h + 1] for x in stats]
            top = jnp.maximum(jnp.maximum(ms[0], ms[1]), ms[2])
            ex = [jnp.exp2(x - top) for x in ms]
            inv = 1.0 / (ex[0] * ls[0] + ex[1] * ls[1] + ex[2] * ls[2])
            wts.append([x * inv for x in ex])
        vals = (o1_ref[:, cols].astype(F32), on_ref[j], on_ref[nslab + j])
        acc = None
        for gi in range(3):
            term = jnp.where(low, wts[0][gi], wts[1][gi]) * vals[gi]
            acc = term if acc is None else acc + term
        out_ref[:, cols] = acc.astype(BF16)


def _s5_kernel(u_ref, wbr_ref, wbi_ref, ar_ref, ai_ref, cr_ref, ci_ref, dsk_ref, wglu_ref, bglu_ref,
               out_ref, h_ref, st_ref, utb_ref, uslab_ref, oslab_ref, g_ref):
    nst = SSM_STATES
    nb, tc, W = u_ref.shape
    rows = nb * tc
    pitch = S5_PITCH
    lanes = V7X_LANES
    wslab = W // lanes
    tile = V7X_MXU_DIM
    chan_tile_states = tile * SSM_STATE // SSM_GROUP
    g = pl.program_id(0)
    buf_a = lax.rem(g, 3)
    buf_b = lax.rem(g + 2, 3)
    buf_c = lax.rem(g + 1, 3)

    @pl.when(g == 0)
    def _():
        h_ref[...] = jnp.zeros_like(h_ref)
        st_ref[...] = jnp.zeros_like(st_ref)
        utb_ref[...] = jnp.zeros_like(utb_ref)

    for s in range(wslab):
        for b in range(nb):
            uslab_ref[s, b * pitch:b * pitch + tc, :] = u_ref[b, :, s * lanes:(s + 1) * lanes].astype(F32)
    for s in range(wslab):
        for t in range(0, tc, 2):
            pair = jnp.concatenate([uslab_ref[s, pl.ds(t, nb, stride=pitch), :],
                                    uslab_ref[s, pl.ds(t + 1, nb, stride=pitch), :]], axis=0)
            utb_ref[buf_a, t * nb:(t + 2) * nb, s * lanes:(s + 1) * lanes] = pair.astype(BF16)

    pieces = []

    def stage_c(n):
        def run():
            s0 = n * chan_tile_states
            cols = slice(n * tile, (n + 1) * tile)
            yr = _dot(st_ref[buf_c, :, s0:s0 + chan_tile_states].astype(BF16), cr_ref[s0:s0 + chan_tile_states, cols])
            yi = _dot(st_ref[buf_c, :, nst + s0:nst + s0 + chan_tile_states].astype(BF16),
                      ci_ref[s0:s0 + chan_tile_states, cols])
            y = yr - yi + utb_ref[buf_c, :, cols].astype(F32) * dsk_ref[:, cols]
            g_ref[:, cols] = _gelu(y)
        return run

    def stage_c_out():
        z = _dot(g_ref[...].astype(BF16), wglu_ref[...]) + bglu_ref[...]
        g_ref[...] = g_ref[...] * _sigmoid(z)
        for s in range(wslab):
            for t in range(tc):
                oslab_ref[s, pl.ds(t, nb, stride=pitch), :] = g_ref[t * nb:(t + 1) * nb, s * lanes:(s + 1) * lanes]
        for s in range(wslab):
            for b in range(nb):
                out_ref[b, :, s * lanes:(s + 1) * lanes] = oslab_ref[s, b * pitch:b * pitch + tc, :].astype(BF16)

    def stage_a(j, w_ref, off):
        def run():
            kt = (j * tile // chan_tile_states) * tile
            st_ref[buf_a, :, off + j * tile:off + (j + 1) * tile] = _dot(
                utb_ref[buf_a, :, kt:kt + tile], w_ref[kt:kt + tile, j * tile:(j + 1) * tile])
        return run

    pieces += [stage_c(n) for n in range(W // tile)] + [stage_c_out]
    for j in range(nst // tile):
        pieces += [stage_a(j, wbr_ref, 0), stage_a(j, wbi_ref, nst)]

    cw = S5_LANE_CHUNK
    nchunk = nst // cw
    every = (nchunk * tc) // len(pieces)
    slot = 0
    for c in range(nchunk):
        lo = c * cw
        ar = jnp.broadcast_to(ar_ref[:, lo:lo + cw], (nb, cw))
        ai = jnp.broadcast_to(ai_ref[:, lo:lo + cw], (nb, cw))
        hr = h_ref[:, lo:lo + cw]
        hi = h_ref[:, nst + lo:nst + lo + cw]
        for t in range(tc):
            r = slice(t * nb, (t + 1) * nb)
            br = st_ref[buf_b, r, lo:lo + cw]
            bi = st_ref[buf_b, r, nst + lo:nst + lo + cw]
            hr, hi = ar * hr - ai * hi + br, ar * hi + ai * hr + bi
            st_ref[buf_b, r, lo:lo + cw] = hr
            st_ref[buf_b, r, nst + lo:nst + lo + cw] = hi
            slot += 1
            if slot % every == 0 and pieces:
                pieces.pop(0)()
        h_ref[:, lo:lo + cw] = hr
        h_ref[:, nst + lo:nst + lo + cw] = hi
    for piece in pieces:
        piece()


def _s5(u, l, wbr, wbi, ar, ai, cr, ci, dskip, wglu, bglu):
    B, S, W = u.shape
    tc = S5_STEPS
    n = S // tc
    return pl.pallas_call(
        _s5_kernel,
        grid=(n + 2,),
        in_specs=[pl.BlockSpec((B, tc, W), lambda g: (0, jnp.minimum(g, n - 1), 0))]
        + [_const_spec(a.shape) for a in (wbr, wbi, ar, ai, cr, ci, dskip)]
        + [_layer_spec(wglu.shape, l), _const_spec(bglu.shape)],
        out_specs=pl.BlockSpec((B, tc, W), lambda g: (0, jnp.maximum(g - 2, 0), 0)),
        out_shape=jax.ShapeDtypeStruct((B, S, W), BF16),
        scratch_shapes=[
            pltpu.VMEM((B, 2 * SSM_STATES), F32),
            pltpu.VMEM((3, B * tc, 2 * SSM_STATES), F32),
            pltpu.VMEM((3, B * tc, W), BF16),
            pltpu.VMEM((W // V7X_LANES, B * S5_PITCH, V7X_LANES), F32),
            pltpu.VMEM((W // V7X_LANES, B * S5_PITCH, V7X_LANES), F32),
            pltpu.VMEM((B * tc, W), F32),
        ],
        compiler_params=_params("arbitrary"),
        name="s5",
    )(u, wbr, wbi, ar, ai, cr, ci, dskip, wglu, bglu)


def _s5_params(a_re, a_im, log_dt, b_re, b_im, c_re, c_im):
    G, P, C = SSM_GROUPS, SSM_STATE, SSM_GROUP
    lam_re = jnp.minimum(a_re, -1e-4)
    lam_im = a_im
    dt = jnp.exp(log_dt)[:, None]
    mag = jnp.exp(lam_re * dt)
    ab_re, ab_im = mag * jnp.cos(lam_im * dt), mag * jnp.sin(lam_im * dt)
    den = lam_re * lam_re + lam_im * lam_im
    f_re = ((ab_re - 1.0) * lam_re + ab_im * lam_im) / den
    f_im = (ab_im * lam_re - (ab_re - 1.0) * lam_im) / den
    bb_re = f_re[..., None] * b_re - f_im[..., None] * b_im
    bb_im = f_re[..., None] * b_im + f_im[..., None] * b_re
    eye = jnp.eye(G, dtype=F32)

    def in_mat(bb):
        return jnp.einsum('gpc,gh->gchp', bb, eye).reshape(G * C, G * P).astype(BF16)

    def out_mat(cc):
        return jnp.einsum('gcp,gh->gphc', cc, eye).reshape(G * P, G * C).astype(BF16)

    return (in_mat(bb_re), in_mat(bb_im), ab_re.reshape(1, G * P), ab_im.reshape(1, G * P),
            out_mat(c_re), out_mat(c_im))


def _merge_kernel(x_ref, a_ref, o1_ref, o2_ref, o3_ref, s1_ref, s2_ref, s3_ref, c_ref, d_ref,
                  gpre_ref, wg_ref, gb_ref, wup_ref, wout_ref, gpost_ref, out_ref, b_ref, on_ref, sn_ref):
    _attn_mix(o1_ref, o2_ref, o3_ref, s1_ref, s2_ref, s3_ref, b_ref, on_ref, sn_ref)
    x = x_ref[...]
    h = _rms(x, gpre_ref[...]).astype(BF16)
    merged = None
    for i, br in enumerate((a_ref, b_ref, c_ref, d_ref)):
        gate = _sigmoid(_dot(h, wg_ref[:, i * D_MODEL:(i + 1) * D_MODEL]) + gb_ref[i:i + 1, :])
        term = gate * _dot(br[...], wup_ref[i])
        merged = term if merged is None else merged + term
    y = _dot(merged.astype(BF16), wout_ref[...])
    out_ref[...] = x + _rms(y, gpost_ref[...])


def _merge(x, l, a, outs, stats, c, d, gpre, wg, gb, wup, wout, gpost):
    B, S, D = x.shape
    tm = ATT_SPAN
    W = W_BRANCH

    def tok(a_):
        if a_.ndim == 3:
            return pl.BlockSpec((None, tm, a_.shape[-1]), lambda b_, i: (b_, i, 0))
        return pl.BlockSpec((None, None) + a_.shape[2:], lambda b_, i: (b_, i, 0, 0, 0))

    return pl.pallas_call(
        _merge_kernel,
        grid=(B, S // tm),
        in_specs=[tok(a_) for a_ in (x, a, *outs, *stats, c, d)] + [
            _const_spec((1, D)), _layer_spec(wg.shape, l), _const_spec(gb.shape), _layer_spec(wup.shape, l),
            _layer_spec(wout.shape, l), _const_spec((1, D)),
        ],
        out_specs=tok(x),
        out_shape=jax.ShapeDtypeStruct((B, S, D), F32),
        scratch_shapes=[
            pltpu.VMEM((tm, W), BF16),
            pltpu.VMEM((2 * W // V7X_LANES, tm, V7X_LANES), F32),
            pltpu.VMEM((2, tm, V7X_LANES), F32),
        ],
        compiler_params=_params("parallel", "parallel"),
        name="merge",
    )(x, a, *outs, *stats, c, d, gpre, wg, gb, wup, wout, gpost)


def _mem_kv_kernel(mem_ref, g_ref, w_ref, k_ref, v_ref):
    mn = _rms(mem_ref[...], g_ref[...]).astype(BF16)
    hw = X_HEADS * X_HEAD_DIM
    k_ref[...] = _dot(mn, w_ref[:, :hw]).astype(BF16)
    v_ref[...] = _dot(mn, w_ref[:, hw:]).astype(BF16)


def _mem_kv(mem, l, g, w):
    B, M, D = mem.shape
    hw = X_HEADS * X_HEAD_DIM
    ob = pl.BlockSpec((None, M, hw), lambda b: (b, 0, 0))
    return pl.pallas_call(
        _mem_kv_kernel,
        grid=(B,),
        in_specs=[pl.BlockSpec((None, M, D), lambda b: (b, 0, 0)), _const_spec((1, D)), _layer_spec(w.shape, l)],
        out_specs=[ob, ob],
        out_shape=[jax.ShapeDtypeStruct((B, M, hw), BF16)] * 2,
        compiler_params=_params("parallel"),
        name="mem_kv",
    )(mem, g, w)


def _xattn_ffn_kernel(x_ref, k_ref, v_ref, gxpre_ref, wq_ref, wo_ref, gxpost_ref, gfpre_ref, w1_ref, w2_ref,
                      gfpost_ref, out_ref, o_ref, acc_ref):
    x = x_ref[...]
    h = _rms(x, gxpre_ref[...]).astype(BF16)
    q = (_dot(h, wq_ref[...]) * (X_HEAD_DIM ** -0.5)).astype(BF16)
    heads = [slice(hd * X_HEAD_DIM, (hd + 1) * X_HEAD_DIM) for hd in range(X_HEADS)]
    scores = [lax.dot_general(q[:, cols], k_ref[:, cols], (((1,), (1,)), ((), ())), preferred_element_type=F32)
              for cols in heads]
    for cols, s in zip(heads, scores):
        m = jnp.max(s, axis=-1, keepdims=True)
        p = jnp.exp(s - m)
        l = jnp.sum(p, axis=-1, keepdims=True)
        o_ref[:, cols] = (_dot(p.astype(BF16), v_ref[:, cols]) * (1.0 / l)).astype(BF16)
    x = x + _rms(_dot(o_ref[...], wo_ref[...]), gxpost_ref[...])
    h = _rms(x, gfpre_ref[...]).astype(BF16)
    for c in range(D_FF // FF_CHUNK):
        cols = slice(c * FF_CHUNK, (c + 1) * FF_CHUNK)
        a = jnp.maximum(_dot(h, w1_ref[:, cols]), 0.0)
        a = (a * a).astype(BF16)
        part = _dot(a, w2_ref[cols, :])
        if c == 0:
            acc_ref[...] = part
        else:
            acc_ref[...] += part
    out_ref[...] = x + _rms(acc_ref[...], gfpost_ref[...])


def _xattn_ffn(x, l, k, v, gxpre, wq, wo, gxpost, gfpre, w1, w2, gfpost):
    B, S, D = x.shape
    tm = TOKEN_BLOCK
    hw = X_HEADS * X_HEAD_DIM
    xb = pl.BlockSpec((None, tm, D), lambda b, i: (b, i, 0))
    kb = pl.BlockSpec((None, N_MEM, hw), lambda b, i: (b, 0, 0))
    consts = (gxpre, wq, wo, gxpost, gfpre, w1, w2, gfpost)
    return pl.pallas_call(
        _xattn_ffn_kernel,
        grid=(B, S // tm),
        in_specs=[xb, kb, kb] + [_layer_spec(a.shape, l) if a.ndim == 3 else _const_spec(a.shape) for a in consts],
        out_specs=xb,
        out_shape=jax.ShapeDtypeStruct((B, S, D), F32),
        scratch_shapes=[pltpu.VMEM((tm, hw), BF16), pltpu.VMEM((tm, D), F32)],
        compiler_params=_params("parallel", "parallel"),
        name="xattn_ffn",
    )(x, k, v, *consts)


def _t5_bucket(n):
    exact = REL_BUCKETS // 2
    nf = np.maximum(n, 1).astype(np.float32)
    large = exact + (np.log(nf / exact) / np.log(REL_MAX_DIST / exact) * (REL_BUCKETS - exact)).astype(np.int32)
    large = np.minimum(large, REL_BUCKETS - 1)
    return np.where(n < exact, n, large).astype(np.int32)


def _band_bias(rel_bias, g, band, dil):
    d = np.arange(band + 1)
    table = rel_bias[jnp.asarray(_t5_bucket(d * dil))][:, g * ATT_HEADS:(g + 1) * ATT_HEADS].astype(F32) * LOG2E
    pad = jnp.full((ATT_HEADS, band - 1), NEG_INF, F32)
    f = jnp.concatenate([pad, table[::-1].T, pad], axis=1)
    full = jnp.stack([f[:, band - 1 - i:3 * band - 1 - i] for i in range(band)], axis=1)
    no_prev = jnp.concatenate([jnp.full((ATT_HEADS, band, band), NEG_INF, F32), full[:, :, band:]], axis=2)
    return jnp.stack([full, no_prev], axis=0)


def kernel(x, mem, rel_bias, g_mix_pre, g_mix_post, w_in, gate_b, pool_w, pool_scale, a_re, a_im, log_dt,
           b_re, b_im, c_re, c_im, d_skip, w_glu, b_glu, sgu_ln_g, sgu_ln_b, w_s, b_s, w_up, w_out,
           g_x_pre, g_x_post, g_mem, w_cq, w_ckv, w_co, g_ff_pre, g_ff_post, w_ff1, w_ff2):
    B, S, D = x.shape
    depth = w_in.shape[0]
    assert D == D_MODEL and S % (TOKEN_BLOCK) == 0 and B == V7X_SUBLANES
    for win, dil in DIL_GROUPS:
        assert win // dil == ATT_BAND and (S // dil) % ATT_BAND == 0

    biases = [_band_bias(rel_bias, g, win // dil, dil) for g, (win, dil) in enumerate(DIL_GROUPS)]
    row = lambda a: a.reshape(1, -1).astype(F32)
    w_mix, w_gate = w_in[:, :, :OFF_GATE].astype(BF16), w_in[:, :, OFF_GATE:].astype(BF16)
    w_up, w_out, w_glu, w_cq, w_ckv, w_co, w_ff1, w_ff2 = (
        a.astype(BF16) for a in (w_up, w_out, w_glu, w_cq, w_ckv, w_co, w_ff1, w_ff2))

    for l in range(depth):
        a_out, qkv1, qkv2, qkv3, ssm_in, d_out = _in_proj(
            x, l, row(g_mix_pre[l]), w_mix, pool_w[l].astype(BF16), row(pool_scale[l]),
            row(sgu_ln_g[l]), row(sgu_ln_b[l]), w_s[l], b_s[l].T)
        outs, stats = [], []
        for qkv, bias, (win, dil) in zip((qkv1, qkv2, qkv3), biases, DIL_GROUPS):
            o, st = _attn_group(qkv, bias, dil)
            outs.append(o)
            stats.append(st)
        s5p = _s5_params(a_re[l], a_im[l], log_dt[l], b_re[l], b_im[l], c_re[l], c_im[l])
        c_out = _s5(ssm_in, l, *s5p, row(d_skip[l]), w_glu, row(b_glu[l]))
        x = _merge(x, l, a_out, outs, stats, c_out, d_out, row(g_mix_pre[l]), w_gate, gate_b[l], w_up, w_out,
                   row(g_mix_post[l]))
        k_mem, v_mem = _mem_kv(mem, l, row(g_mem[l]), w_ckv)
        x = _xattn_ffn(x, l, k_mem, v_mem, row(g_x_pre[l]), w_cq, w_co, row(g_x_post[l]),
                       row(g_ff_pre[l]), w_ff1, w_ff2, row(g_ff_post[l]))
    return x
```

```python
import functools
import math

import jax
import jax.numpy as jnp
import numpy as np
from jax import lax
from jax.experimental import pallas as pl
from jax.experimental.pallas import tpu as pltpu

F32 = jnp.float32
BF16 = jnp.bfloat16

D_MODEL = 1024
N_MEM = 256
N_BRANCH = 4
W_BRANCH = D_MODEL // 2
POOL_WINDOWS = (2, 4, 8, 16)
POOL_GROUP = W_BRANCH // len(POOL_WINDOWS)
DIL_GROUPS = ((128, 1), (512, 4), (2048, 16))
ATT_HEADS = 8
ATT_HEAD_DIM = W_BRANCH // ATT_HEADS
SSM_GROUP = 16
SSM_GROUPS = W_BRANCH // SSM_GROUP
SSM_STATE = 64
SSM_STATES = SSM_GROUPS * SSM_STATE
SGU_CHUNK = 128
SGU_GROUPS = 4
X_HEADS = 4
X_HEAD_DIM = 128
D_FF = 4 * D_MODEL
REL_BUCKETS = 32
REL_MAX_DIST = 2048
EPS = 1e-6
NEG_INF = -1e30
N_ATT_COLS = 3 * len(DIL_GROUPS) * W_BRANCH
OFF_GATE = W_BRANCH + N_ATT_COLS + W_BRANCH + 2 * W_BRANCH

V7X_LANES = 128
V7X_SUBLANES = 8
V7X_MXU_DIM = 256
V7X_VMEM_LIMIT_BYTES = 56 * 1024 * 1024

TOKEN_BLOCK = 1024
POOL_HALO = 16
ATT_BAND = 128
ATT_SPAN = 512
ATT_UNITS = 4
IN_PROJ_SLAB_SETS = 3
LOG2E = 1.4426950408889634
ATT_Q_SCALE = ATT_HEAD_DIM ** -0.5 * LOG2E
S5_STEPS = 64
S5_PITCH = S5_STEPS + 8
S5_LANE_CHUNK = 512
FF_CHUNK = 512


def _params(*sem):
    return pltpu.CompilerParams(dimension_semantics=sem, vmem_limit_bytes=V7X_VMEM_LIMIT_BYTES)


def _const_spec(shape):
    nd = len(shape)
    return pl.BlockSpec(shape, lambda *_: (0,) * nd, pipeline_mode=pl.Buffered(1))


def _layer_spec(shape, l):
    nd = len(shape) - 1
    return pl.BlockSpec((None,) + tuple(shape[1:]), lambda *_: (l,) + (0,) * nd, pipeline_mode=pl.Buffered(1))


def _rms(xf, g):
    return xf * lax.rsqrt(jnp.mean(xf * xf, axis=-1, keepdims=True) + EPS) * g


def _gelu(x):
    c = math.sqrt(2.0 / math.pi)
    return 0.5 * x * (1.0 + jnp.tanh(c * (x + 0.044715 * (x * x * x))))


def _sigmoid(x):
    return 1.0 / (1.0 + jnp.exp(-x))


def _dot(a, b):
    return jnp.dot(a, b, preferred_element_type=F32)


def _in_proj_kernel(x_ref, xh_ref, g_ref, w_ref, poolw_ref, pscale_ref, lng_ref, lnb_ref, ws_ref, bst_ref,
                    a_ref, qkv1_ref, qkv2_ref, qkv3_ref, ssm_ref, d_ref, slab_ref, ext_ref):
    h = _rms(x_ref[...], g_ref[...]).astype(BF16)
    tm = x_ref.shape[0]
    step = W_BRANCH
    ng = len(DIL_GROUPS)
    nslab = step // V7X_LANES
    i = pl.program_id(1)
    proj = lambda lo: _dot(h, w_ref[:, lo:lo + step])

    lo_sgu = W_BRANCH + N_ATT_COLS + W_BRANCH
    u = _gelu(proj(lo_sgu))
    v = _gelu(proj(lo_sgu + step))
    vc = v - jnp.mean(v, axis=-1, keepdims=True)
    var = jnp.mean(vc * vc, axis=-1, keepdims=True)
    vn = (vc * lax.rsqrt(var + EPS) * lng_ref[...] + lnb_ref[...]).astype(BF16)

    halo = _dot(_rms(xh_ref[...], g_ref[...]).astype(BF16), w_ref[:, 0:step])
    ext_ref[0:POOL_HALO, :] = jnp.where(i == 0, 0.0, halo)
    ext_ref[POOL_HALO:, :] = proj(0)

    def qkv(which):
        dst = slice(which * step, (which + 1) * step)
        for g, out_ref in enumerate((qkv1_ref, qkv2_ref, qkv3_ref)):
            y = proj(W_BRANCH + (which * ng + g) * step)
            if which == 0:
                y = y * ATT_Q_SCALE
            dil = DIL_GROUPS[g][1]
            if dil == 1:
                out_ref[:, dst] = y.astype(BF16)
                continue
            base = ((which * (ng - 1) + g - 1) % IN_PROJ_SLAB_SETS) * nslab
            for c in range(nslab):
                slab_ref[base + c] = y[:, c * V7X_LANES:(c + 1) * V7X_LANES]
            for c in range(nslab):
                cols = slice(which * step + c * V7X_LANES, which * step + (c + 1) * V7X_LANES)
                for r in range(dil):
                    out_ref[r, :, cols] = slab_ref[base + c, pl.ds(r, tm // dil, stride=dil), :].astype(BF16)

    qkv(0)

    tpos = i * tm + lax.broadcasted_iota(jnp.int32, (tm, 1), 0)
    for gi, win in enumerate(POOL_WINDOWS):
        lanes = slice(gi * POOL_GROUP, (gi + 1) * POOL_GROUP)
        tok = ext_ref[POOL_HALO:POOL_HALO + tm, lanes]
        s = tok
        for j in range(1, win):
            s = s + ext_ref[POOL_HALO - j:POOL_HALO - j + tm, lanes]
        cnt = jnp.minimum(tpos + 1, win).astype(F32)
        p = s / cnt - tok
        a_ref[:, lanes] = (_dot(p.astype(BF16), poolw_ref[gi]) * pscale_ref[:, lanes]).astype(BF16)

    qkv(1)

    T = SGU_CHUNK
    gd = W_BRANCH // SGU_GROUPS
    causal = lax.broadcasted_iota(jnp.int32, (T, T), 0) >= lax.broadcasted_iota(jnp.int32, (T, T), 1)
    for g in range(SGU_GROUPS):
        wg = jnp.where(causal, ws_ref[g], 0.0).astype(BF16)
        bias = bst_ref[:, g:g + 1]
        cols = slice(g * gd, (g + 1) * gd)
        for c in range(tm // T):
            rows = slice(c * T, (c + 1) * T)
            sv = _dot(wg, vn[rows, cols]) + bias
            d_ref[rows, cols] = (u[rows, cols] * sv).astype(BF16)

    qkv(2)
    ssm_ref[...] = proj(W_BRANCH + N_ATT_COLS).astype(BF16)


def _residue_major_shape(B, S, dil, width):
    return (B, S // ATT_SPAN, dil, ATT_SPAN // dil, width)


def _in_proj(x, l, g, w, pool_w, pool_scale, ln_g, ln_b, w_s, b_s_t):
    B, S, D = x.shape
    tm = ATT_SPAN
    qw = 3 * W_BRANCH
    d2, d3 = DIL_GROUPS[1][1], DIL_GROUPS[2][1]
    tok = lambda width: pl.BlockSpec((None, tm, width), lambda b, i: (b, i, 0))
    return pl.pallas_call(
        _in_proj_kernel,
        grid=(B, S // tm),
        in_specs=[
            tok(D),
            pl.BlockSpec((None, POOL_HALO, D), lambda b, i: (b, jnp.maximum(i * (tm // POOL_HALO) - 1, 0), 0)),
            _const_spec((1, D)),
            _layer_spec(w.shape, l),
            _const_spec(pool_w.shape), _const_spec(pool_scale.shape), _const_spec(ln_g.shape),
            _const_spec(ln_b.shape), _const_spec(w_s.shape), _const_spec(b_s_t.shape),
        ],
        out_specs=[
            tok(W_BRANCH),
            tok(qw),
            pl.BlockSpec((None, None, d2, tm // d2, qw), lambda b, i: (b, i, 0, 0, 0)),
            pl.BlockSpec((None, None, d3, tm // d3, qw), lambda b, i: (b, i, 0, 0, 0)),
            tok(W_BRANCH),
            tok(W_BRANCH),
        ],
        out_shape=[
            jax.ShapeDtypeStruct((B, S, W_BRANCH), BF16),
            jax.ShapeDtypeStruct((B, S, qw), BF16),
            jax.ShapeDtypeStruct(_residue_major_shape(B, S, d2, qw), BF16),
            jax.ShapeDtypeStruct(_residue_major_shape(B, S, d3, qw), BF16),
            jax.ShapeDtypeStruct((B, S, W_BRANCH), BF16),
            jax.ShapeDtypeStruct((B, S, W_BRANCH), BF16),
        ],
        scratch_shapes=[
            pltpu.VMEM((IN_PROJ_SLAB_SETS * W_BRANCH // V7X_LANES, tm, V7X_LANES), F32),
            pltpu.VMEM((POOL_HALO + tm, W_BRANCH), F32),
        ],
        compiler_params=_params("parallel", "parallel"),
        name="in_proj",
    )(x, x, g, w, pool_w, pool_scale, ln_g, ln_b, w_s, b_s_t)


def _attn_tasks(layout, first_step, q_ref, kp_ref, kc_ref, vp_ref, vc_ref, bias_ref, o_ref, st_ref):
    band, W = ATT_BAND, W_BRANCH
    lane = lax.broadcasted_iota(jnp.int32, (1, V7X_LANES), 1)
    low = lane < ATT_HEAD_DIM
    cache = {}

    def unit(u):
        if u in cache:
            return cache[u]
        table = jnp.where(first_step, 1, 0)
        if layout == "natural":
            if "ext" not in cache:
                cache["ext"] = (jnp.concatenate([kp_ref[...], kc_ref[...]], axis=0),
                                jnp.concatenate([vp_ref[...], vc_ref[...]], axis=0))
            k_ext, v_ext = cache["ext"]
            rows = slice(u * band, (u + 1) * band)
            cache[u] = (q_ref[rows, :], k_ext[u * band:(u + 2) * band], v_ext[u * band:(u + 2) * band],
                        table if u == 0 else 0,
                        lambda cols, val: o_ref.__setitem__((rows, cols), val),
                        lambda val: st_ref.__setitem__((rows, slice(None)), val))
        elif layout == "span":
            cat = lambda p, c: jnp.concatenate([p[u], c[u]], axis=0)
            cache[u] = (q_ref[u], cat(kp_ref, kc_ref), cat(vp_ref, vc_ref), table,
                        lambda cols, val: o_ref.__setitem__((u, slice(None), cols), val),
                        lambda val: st_ref.__setitem__((u,), val))
        else:
            sp, per = q_ref.shape[0], q_ref.shape[2]
            get = lambda ref: ref[:, u].reshape(band, W)
            cat = lambda p, c: jnp.concatenate([get(p), get(c)], axis=0)
            cache[u] = (get(q_ref), cat(kp_ref, kc_ref), cat(vp_ref, vc_ref), table,
                        lambda cols, val: o_ref.__setitem__((slice(None), u, slice(None), cols),
                                                            val.reshape(sp, per, V7X_LANES)),
                        lambda val: st_ref.__setitem__((slice(None), u), val.reshape(sp, per, V7X_LANES)))
        return cache[u]

    def score(u, j):
        q, k, _, table, _, _ = unit(u)
        cols = slice(j * V7X_LANES, (j + 1) * V7X_LANES)
        qp = q[:, cols]
        zq = jnp.zeros_like(qp)
        q2 = jnp.concatenate([jnp.where(low, qp, zq), jnp.where(low, zq, qp)], axis=0)
        s = lax.dot_general(q2, k[:, cols], (((1,), (1,)), ((), ())), preferred_element_type=F32)
        return s + jnp.concatenate([bias_ref[table, 2 * j], bias_ref[table, 2 * j + 1]], axis=0)

    todo = [(u, j) for u in range(ATT_UNITS) for j in range(ATT_HEADS // 2)]
    live = {}

    def task(idx):
        def run():
            u, j = todo[idx]
            _, _, v, _, put_o, put_st = unit(u)
            s = live.pop("score") if "score" in live else score(u, j)
            if idx + 1 < len(todo):
                live["score"] = score(*todo[idx + 1])
            stats = jnp.zeros((band, V7X_LANES), F32) if j == 0 else live.pop("stats")
            cols = slice(j * V7X_LANES, (j + 1) * V7X_LANES)
            m = jnp.max(s, axis=-1, keepdims=True)
            p = jnp.exp2(s - m)
            l = jnp.sum(p, axis=-1, keepdims=True)
            pb = p.astype(BF16)
            p2 = jnp.concatenate([pb[:band], pb[band:]], axis=1)
            vp = v[:, cols]
            zv = jnp.zeros_like(vp)
            v2 = jnp.concatenate([jnp.where(low, vp, zv), jnp.where(low, zv, vp)], axis=0)
            put_o(cols, _dot(p2, v2).astype(BF16))
            for e in range(2):
                h = 2 * j + e
                stats = jnp.where(lane == h, m[e * band:(e + 1) * band], stats)
                stats = jnp.where(lane == ATT_HEADS + h, l[e * band:(e + 1) * band], stats)
            if j == ATT_HEADS // 2 - 1:
                put_st(stats)
            else:
                live["stats"] = stats
        return run

    return [task(idx) for idx in range(len(todo))]


def _attn_specs(qkv, dil, n_items):
    B = qkv.shape[0]
    W = W_BRANCH
    U = ATT_UNITS
    item = lambda g: jnp.minimum(g, n_items - 1)
    if dil == 1:
        nspan = qkv.shape[1] // (U * ATT_BAND)
        layout = "natural"
        blk = lambda width: (None, U * ATT_BAND, width)
        prev_blk = (None, ATT_BAND, W)
        pos = lambda g: (item(g) // nspan, item(g) % nspan)
        first = lambda g: pos(g)[1] == 0
        idx = lambda which: (lambda g: (*pos(g), which))
        idx_prev = lambda which: (lambda g: (pos(g)[0], jnp.maximum(U * pos(g)[1] - 1, 0), which))
        out_shape = lambda width: (B, qkv.shape[1], width)
    else:
        nspan, _, per, _ = qkv.shape[1:]
        spans = ATT_BAND // per
        rgroups = dil // U
        nband = nspan // spans
        layout = "span" if spans == 1 else "band"
        blk = lambda width: ((None, None, U, per, width) if spans == 1 else (None, spans, U, per, width))
        prev_blk = blk(W)
        pos = lambda g: (item(g) // (nband * rgroups), (item(g) // rgroups) % nband, item(g) % rgroups)
        first = lambda g: pos(g)[1] == 0
        idx = lambda which: (lambda g: (*pos(g), 0, which))
        idx_prev = lambda which: (lambda g: (pos(g)[0], jnp.maximum(pos(g)[1] - 1, 0), pos(g)[2], 0, which))
        out_shape = lambda width: (B, nspan, dil, per, width)
        assert B * nband * rgroups == n_items
    in_specs = [pl.BlockSpec(blk(W), idx(0)), pl.BlockSpec(prev_blk, idx_prev(1)), pl.BlockSpec(blk(W), idx(1)),
                pl.BlockSpec(prev_blk, idx_prev(2)), pl.BlockSpec(blk(W), idx(2))]
    return layout, first, in_specs, (lambda width: pl.BlockSpec(blk(width), idx(0))), out_shape


def _attn_kernel(layout, first, *refs):
    for task in _attn_tasks(layout, first(pl.program_id(0)), *refs):
        task()


def _attn_group(qkv, bias, dil):
    n_items = qkv.shape[0] * (qkv.shape[1] if qkv.ndim == 5 else qkv.shape[1] // ATT_SPAN)
    layout, first, in_specs, out_spec, out_shape = _attn_specs(qkv, dil, n_items)
    return pl.pallas_call(
        functools.partial(_attn_kernel, layout, first),
        grid=(n_items,),
        in_specs=in_specs + [_const_spec(bias.shape)],
        out_specs=[out_spec(W_BRANCH), out_spec(V7X_LANES)],
        out_shape=[jax.ShapeDtypeStruct(out_shape(W_BRANCH), BF16),
                   jax.ShapeDtypeStruct(out_shape(V7X_LANES), F32)],
        compiler_params=_params("parallel"),
        name=f"attn_d{dil}",
    )(qkv, qkv, qkv, qkv, qkv, bias)


def _attn_mix(o1_ref, o2_ref, o3_ref, s1_ref, s2_ref, s3_ref, out_ref, on_ref, sn_ref):
    lane = lax.broadcasted_iota(jnp.int32, (1, V7X_LANES), 1)
    low = lane < ATT_HEAD_DIM
    nslab = W_BRANCH // V7X_LANES
    for gi, (o_ref, s_ref) in enumerate(((o2_ref, s2_ref), (o3_ref, s3_ref))):
        dil, per = o_ref.shape[0], o_ref.shape[1]
        for r in range(dil):
            sn_ref[gi, pl.ds(r, per, stride=dil), :] = s_ref[r]
            for c in range(nslab):
                on_ref[gi * nslab + c, pl.ds(r, per, stride=dil), :] = (
                    o_ref[r, :, c * V7X_LANES:(c + 1) * V7X_LANES].astype(F32))
    stats = (s1_ref[...], sn_ref[0], sn_ref[1])
    for j in range(ATT_HEADS // 2):
        cols = slice(j * V7X_LANES, (j + 1) * V7X_LANES)
        wts = []
        for e in range(2):
            h = 2 * j + e
            ms = [x[:, h:h + 1] for x in stats]
            ls = [x[:, ATT_HEADS + h:ATT_HEADS + h + 1] for x in stats]
            top = jnp.maximum(jnp.maximum(ms[0], ms[1]), ms[2])
            ex = [jnp.exp2(x - top) for x in ms]
            inv = 1.0 / (ex[0] * ls[0] + ex[1] * ls[1] + ex[2] * ls[2])
            wts.append([x * inv for x in ex])
        vals = (o1_ref[:, cols].astype(F32), on_ref[j], on_ref[nslab + j])
        acc = None
        for gi in range(3):
            term = jnp.where(low, wts[0][gi], wts[1][gi]) * vals[gi]
            acc = term if acc is None else acc + term
        out_ref[:, cols] = acc.astype(BF16)


def _s5_kernel(*refs):
    n_io = 11
    h_ref, st_even, st_odd, utb_even, utb_odd = refs[n_io:n_io + 5]
    io, shared = refs[:n_io], refs[n_io + 5:]
    g = pl.program_id(0)

    @pl.when(g == 0)
    def _():
        h_ref[...] = jnp.zeros_like(h_ref)
        for ref in (st_even, st_odd, utb_even, utb_odd):
            ref[...] = jnp.zeros_like(ref)

    @pl.when(lax.rem(g, 2) == 0)
    def _():
        _s5_step(h_ref, st_even, st_odd, utb_even, *io, *shared)

    @pl.when(lax.rem(g, 2) == 1)
    def _():
        _s5_step(h_ref, st_odd, st_even, utb_odd, *io, *shared)


def _s5_step(h_ref, st_ac, st_b, utb_ac, u_ref, wbr_ref, wbi_ref, ar_ref, ai_ref, cr_ref, ci_ref, dsk_ref,
             wglu_ref, bglu_ref, out_ref, uslab_ref, oslab_ref, g_ref):
    nst = SSM_STATES
    nb, tc, W = u_ref.shape
    pitch = S5_PITCH
    lanes = V7X_LANES
    wslab = W // lanes
    tile = V7X_MXU_DIM
    chan_tile_states = tile * SSM_STATE // SSM_GROUP

    def stage_a_relayout():
        for s in range(wslab):
            for b in range(nb):
                uslab_ref[s, b * pitch:b * pitch + tc, :] = u_ref[b, :, s * lanes:(s + 1) * lanes].astype(F32)
        for s in range(wslab):
            for t in range(0, tc, 2):
                pair = jnp.concatenate([uslab_ref[s, pl.ds(t, nb, stride=pitch), :],
                                        uslab_ref[s, pl.ds(t + 1, nb, stride=pitch), :]], axis=0)
                utb_ac[t * nb:(t + 2) * nb, s * lanes:(s + 1) * lanes] = pair.astype(BF16)

    pieces = []

    def stage_c(n):
        def run():
            s0 = n * chan_tile_states
            cols = slice(n * tile, (n + 1) * tile)
            yr = _dot(st_ac[:, s0:s0 + chan_tile_states].astype(BF16), cr_ref[n])
            yi = _dot(st_ac[:, nst + s0:nst + s0 + chan_tile_states].astype(BF16), ci_ref[n])
            y = yr - yi + utb_ac[:, cols].astype(F32) * dsk_ref[:, cols]
            g_ref[:, cols] = _gelu(y)
        return run

    def stage_c_out():
        z = _dot(g_ref[...].astype(BF16), wglu_ref[...]) + bglu_ref[...]
        g_ref[...] = g_ref[...] * _sigmoid(z)
        for s in range(wslab):
            for t in range(tc):
                oslab_ref[s, pl.ds(t, nb, stride=pitch), :] = g_ref[t * nb:(t + 1) * nb, s * lanes:(s + 1) * lanes]
        for s in range(wslab):
            for b in range(nb):
                out_ref[b, :, s * lanes:(s + 1) * lanes] = oslab_ref[s, b * pitch:b * pitch + tc, :].astype(BF16)

    def stage_a(j, w_ref, off):
        def run():
            kt = (j * tile // chan_tile_states) * tile
            st_ac[:, off + j * tile:off + (j + 1) * tile] = _dot(utb_ac[:, kt:kt + tile], w_ref[j])
        return run

    pieces += [stage_c(n) for n in range(W // tile)] + [stage_c_out, stage_a_relayout]
    for j in range(nst // tile):
        pieces += [stage_a(j, wbr_ref, 0), stage_a(j, wbi_ref, nst)]

    cw = S5_LANE_CHUNK
    nchunk = nst // cw
    every = (nchunk * tc) // len(pieces)
    slot = 0
    for c in range(nchunk):
        lo = c * cw
        ar = jnp.broadcast_to(ar_ref[:, lo:lo + cw], (nb, cw))
        ai = jnp.broadcast_to(ai_ref[:, lo:lo + cw], (nb, cw))
        hr = h_ref[:, lo:lo + cw]
        hi = h_ref[:, nst + lo:nst + lo + cw]
        for t in range(tc):
            r = slice(t * nb, (t + 1) * nb)
            br = st_b[r, lo:lo + cw]
            bi = st_b[r, nst + lo:nst + lo + cw]
            hr, hi = ar * hr - ai * hi + br, ar * hi + ai * hr + bi
            st_b[r, lo:lo + cw] = hr
            st_b[r, nst + lo:nst + lo + cw] = hi
            slot += 1
            if slot % every == 0 and pieces:
                pieces.pop(0)()
        h_ref[:, lo:lo + cw] = hr
        h_ref[:, nst + lo:nst + lo + cw] = hi
    for piece in pieces:
        piece()


def _s5(u, l, wbr, wbi, ar, ai, cr, ci, dskip, wglu, bglu):
    B, S, W = u.shape
    tc = S5_STEPS
    n = S // tc
    return pl.pallas_call(
        _s5_kernel,
        grid=(n + 2,),
        in_specs=[pl.BlockSpec((B, tc, W), lambda g: (0, jnp.minimum(g, n - 1), 0))]
        + [_const_spec(a.shape) for a in (wbr, wbi, ar, ai, cr, ci, dskip)]
        + [_layer_spec(wglu.shape, l), _const_spec(bglu.shape)],
        out_specs=pl.BlockSpec((B, tc, W), lambda g: (0, jnp.maximum(g - 2, 0), 0)),
        out_shape=jax.ShapeDtypeStruct((B, S, W), BF16),
        scratch_shapes=[
            pltpu.VMEM((B, 2 * SSM_STATES), F32),
            pltpu.VMEM((B * tc, 2 * SSM_STATES), F32),
            pltpu.VMEM((B * tc, 2 * SSM_STATES), F32),
            pltpu.VMEM((B * tc, W), BF16),
            pltpu.VMEM((B * tc, W), BF16),
            pltpu.VMEM((W // V7X_LANES, B * S5_PITCH, V7X_LANES), F32),
            pltpu.VMEM((W // V7X_LANES, B * S5_PITCH, V7X_LANES), F32),
            pltpu.VMEM((B * tc, W), F32),
        ],
        compiler_params=_params("arbitrary"),
        name="s5",
    )(u, wbr, wbi, ar, ai, cr, ci, dskip, wglu, bglu)


def _s5_params(a_re, a_im, log_dt, b_re, b_im, c_re, c_im):
    G, P, C = SSM_GROUPS, SSM_STATE, SSM_GROUP
    lam_re = jnp.minimum(a_re, -1e-4)
    lam_im = a_im
    dt = jnp.exp(log_dt)[:, None]
    mag = jnp.exp(lam_re * dt)
    ab_re, ab_im = mag * jnp.cos(lam_im * dt), mag * jnp.sin(lam_im * dt)
    den = lam_re * lam_re + lam_im * lam_im
    f_re = ((ab_re - 1.0) * lam_re + ab_im * lam_im) / den
    f_im = (ab_im * lam_re - (ab_re - 1.0) * lam_im) / den
    bb_re = f_re[..., None] * b_re - f_im[..., None] * b_im
    bb_im = f_re[..., None] * b_im + f_im[..., None] * b_re
    eye = jnp.eye(G, dtype=F32)

    tile = V7X_MXU_DIM
    chan_tile_states = tile * P // C

    def in_mat(bb):
        full = jnp.einsum('gpc,gh->gchp', bb, eye).reshape(G * C, G * P).astype(BF16)
        return jnp.stack([full[(j * tile // chan_tile_states) * tile:(j * tile // chan_tile_states + 1) * tile,
                               j * tile:(j + 1) * tile] for j in range(G * P // tile)])

    def out_mat(cc):
        full = jnp.einsum('gcp,gh->gphc', cc, eye).reshape(G * P, G * C).astype(BF16)
        return jnp.stack([full[n * chan_tile_states:(n + 1) * chan_tile_states, n * tile:(n + 1) * tile]
                          for n in range(G * C // tile)])

    return (in_mat(bb_re), in_mat(bb_im), ab_re.reshape(1, G * P), ab_im.reshape(1, G * P),
            out_mat(c_re), out_mat(c_im))


def _merge_kernel(x_ref, a_ref, o1_ref, o2_ref, o3_ref, s1_ref, s2_ref, s3_ref, c_ref, d_ref,
                  gpre_ref, wg_ref, gb_ref, wup_ref, wout_ref, gpost_ref, out_ref, b_ref, on_ref, sn_ref):
    _attn_mix(o1_ref, o2_ref, o3_ref, s1_ref, s2_ref, s3_ref, b_ref, on_ref, sn_ref)
    x = x_ref[...]
    h = _rms(x, gpre_ref[...]).astype(BF16)
    merged = None
    for i, br in enumerate((a_ref, b_ref, c_ref, d_ref)):
        gate = _sigmoid(_dot(h, wg_ref[:, i * D_MODEL:(i + 1) * D_MODEL]) + gb_ref[i:i + 1, :])
        term = gate * _dot(br[...], wup_ref[i])
        merged = term if merged is None else merged + term
    y = _dot(merged.astype(BF16), wout_ref[...])
    out_ref[...] = x + _rms(y, gpost_ref[...])


def _merge(x, l, a, outs, stats, c, d, gpre, wg, gb, wup, wout, gpost):
    B, S, D = x.shape
    tm = ATT_SPAN
    W = W_BRANCH

    def tok(a_):
        if a_.ndim == 3:
            return pl.BlockSpec((None, tm, a_.shape[-1]), lambda b_, i: (b_, i, 0))
        return pl.BlockSpec((None, None) + a_.shape[2:], lambda b_, i: (b_, i, 0, 0, 0))

    return pl.pallas_call(
        _merge_kernel,
        grid=(B, S // tm),
        in_specs=[tok(a_) for a_ in (x, a, *outs, *stats, c, d)] + [
            _const_spec((1, D)), _layer_spec(wg.shape, l), _const_spec(gb.shape), _layer_spec(wup.shape, l),
            _layer_spec(wout.shape, l), _const_spec((1, D)),
        ],
        out_specs=tok(x),
        out_shape=jax.ShapeDtypeStruct((B, S, D), F32),
        scratch_shapes=[
            pltpu.VMEM((tm, W), BF16),
            pltpu.VMEM((2 * W // V7X_LANES, tm, V7X_LANES), F32),
            pltpu.VMEM((2, tm, V7X_LANES), F32),
        ],
        compiler_params=_params("parallel", "parallel"),
        name="merge",
    )(x, a, *outs, *stats, c, d, gpre, wg, gb, wup, wout, gpost)


def _mem_kv_kernel(mem_ref, g_ref, w_ref, k_ref, v_ref):
    mn = _rms(mem_ref[...], g_ref[...]).astype(BF16)
    hw = X_HEADS * X_HEAD_DIM
    k_ref[...] = _dot(mn, w_ref[:, :hw]).astype(BF16)
    v_ref[...] = _dot(mn, w_ref[:, hw:]).astype(BF16)


def _mem_kv(mem, l, g, w):
    B, M, D = mem.shape
    hw = X_HEADS * X_HEAD_DIM
    ob = pl.BlockSpec((None, M, hw), lambda b: (b, 0, 0))
    return pl.pallas_call(
        _mem_kv_kernel,
        grid=(B,),
        in_specs=[pl.BlockSpec((None, M, D), lambda b: (b, 0, 0)), _const_spec((1, D)), _layer_spec(w.shape, l)],
        out_specs=[ob, ob],
        out_shape=[jax.ShapeDtypeStruct((B, M, hw), BF16)] * 2,
        compiler_params=_params("parallel"),
        name="mem_kv",
    )(mem, g, w)


def _xattn_ffn_kernel(x_ref, k_ref, v_ref, gxpre_ref, wq_ref, wo_ref, gxpost_ref, gfpre_ref, w1_ref, w2_ref,
                      gfpost_ref, out_ref, o_ref, acc_ref):
    x = x_ref[...]
    h = _rms(x, gxpre_ref[...]).astype(BF16)
    q = (_dot(h, wq_ref[...]) * (X_HEAD_DIM ** -0.5)).astype(BF16)
    heads = [slice(hd * X_HEAD_DIM, (hd + 1) * X_HEAD_DIM) for hd in range(X_HEADS)]
    scores = [lax.dot_general(q[:, cols], k_ref[:, cols], (((1,), (1,)), ((), ())), preferred_element_type=F32)
              for cols in heads]
    for cols, s in zip(heads, scores):
        m = jnp.max(s, axis=-1, keepdims=True)
        p = jnp.exp(s - m)
        l = jnp.sum(p, axis=-1, keepdims=True)
        o_ref[:, cols] = (_dot(p.astype(BF16), v_ref[:, cols]) * (1.0 / l)).astype(BF16)
    x = x + _rms(_dot(o_ref[...], wo_ref[...]), gxpost_ref[...])
    h = _rms(x, gfpre_ref[...]).astype(BF16)
    for c in range(D_FF // FF_CHUNK):
        cols = slice(c * FF_CHUNK, (c + 1) * FF_CHUNK)
        a = jnp.maximum(_dot(h, w1_ref[:, cols]), 0.0)
        a = (a * a).astype(BF16)
        part = _dot(a, w2_ref[cols, :])
        if c == 0:
            acc_ref[...] = part
        else:
            acc_ref[...] += part
    out_ref[...] = x + _rms(acc_ref[...], gfpost_ref[...])


def _xattn_ffn(x, l, k, v, gxpre, wq, wo, gxpost, gfpre, w1, w2, gfpost):
    B, S, D = x.shape
    tm = TOKEN_BLOCK
    hw = X_HEADS * X_HEAD_DIM
    xb = pl.BlockSpec((None, tm, D), lambda b, i: (b, i, 0))
    kb = pl.BlockSpec((None, N_MEM, hw), lambda b, i: (b, 0, 0))
    consts = (gxpre, wq, wo, gxpost, gfpre, w1, w2, gfpost)
    return pl.pallas_call(
        _xattn_ffn_kernel,
        grid=(B, S // tm),
        in_specs=[xb, kb, kb] + [_layer_spec(a.shape, l) if a.ndim == 3 else _const_spec(a.shape) for a in consts],
        out_specs=xb,
        out_shape=jax.ShapeDtypeStruct((B, S, D), F32),
        scratch_shapes=[pltpu.VMEM((tm, hw), BF16), pltpu.VMEM((tm, D), F32)],
        compiler_params=_params("parallel", "parallel"),
        name="xattn_ffn",
    )(x, k, v, *consts)


def _t5_bucket(n):
    exact = REL_BUCKETS // 2
    nf = np.maximum(n, 1).astype(np.float32)
    large = exact + (np.log(nf / exact) / np.log(REL_MAX_DIST / exact) * (REL_BUCKETS - exact)).astype(np.int32)
    large = np.minimum(large, REL_BUCKETS - 1)
    return np.where(n < exact, n, large).astype(np.int32)


def _band_bias(rel_bias, g, band, dil):
    d = np.arange(band + 1)
    table = rel_bias[jnp.asarray(_t5_bucket(d * dil))][:, g * ATT_HEADS:(g + 1) * ATT_HEADS].astype(F32) * LOG2E
    pad = jnp.full((ATT_HEADS, band - 1), NEG_INF, F32)
    f = jnp.concatenate([pad, table[::-1].T, pad], axis=1)
    full = jnp.stack([f[:, band - 1 - i:3 * band - 1 - i] for i in range(band)], axis=1)
    no_prev = jnp.concatenate([jnp.full((ATT_HEADS, band, band), NEG_INF, F32), full[:, :, band:]], axis=2)
    return jnp.stack([full, no_prev], axis=0)


def kernel(x, mem, rel_bias, g_mix_pre, g_mix_post, w_in, gate_b, pool_w, pool_scale, a_re, a_im, log_dt,
           b_re, b_im, c_re, c_im, d_skip, w_glu, b_glu, sgu_ln_g, sgu_ln_b, w_s, b_s, w_up, w_out,
           g_x_pre, g_x_post, g_mem, w_cq, w_ckv, w_co, g_ff_pre, g_ff_post, w_ff1, w_ff2):
    B, S, D = x.shape
    depth = w_in.shape[0]
    assert D == D_MODEL and S % (TOKEN_BLOCK) == 0 and B == V7X_SUBLANES
    for win, dil in DIL_GROUPS:
        assert win // dil == ATT_BAND and (S // dil) % ATT_BAND == 0

    biases = [_band_bias(rel_bias, g, win // dil, dil) for g, (win, dil) in enumerate(DIL_GROUPS)]
    row = lambda a: a.reshape(1, -1).astype(F32)
    w_mix, w_gate = w_in[:, :, :OFF_GATE].astype(BF16), w_in[:, :, OFF_GATE:].astype(BF16)
    w_up, w_out, w_glu, w_cq, w_ckv, w_co, w_ff1, w_ff2 = (
        a.astype(BF16) for a in (w_up, w_out, w_glu, w_cq, w_ckv, w_co, w_ff1, w_ff2))

    for l in range(depth):
        a_out, qkv1, qkv2, qkv3, ssm_in, d_out = _in_proj(
            x, l, row(g_mix_pre[l]), w_mix, pool_w[l].astype(BF16), row(pool_scale[l]),
            row(sgu_ln_g[l]), row(sgu_ln_b[l]), w_s[l], b_s[l].T)
        s5p = _s5_params(a_re[l], a_im[l], log_dt[l], b_re[l], b_im[l], c_re[l], c_im[l])
        c_out = _s5(ssm_in, l, *s5p, row(d_skip[l]), w_glu, row(b_glu[l]))
        outs, stats = zip(*[_attn_group(qkv, bias, dil)
                            for qkv, bias, (win, dil) in zip((qkv1, qkv2, qkv3), biases, DIL_GROUPS)])
        x = _merge(x, l, a_out, outs, stats, c_out, d_out, row(g_mix_pre[l]), w_gate, gate_b[l], w_up, w_out,
                   row(g_mix_post[l]))
        k_mem, v_mem = _mem_kv(mem, l, row(g_mem[l]), w_ckv)
        x = _xattn_ffn(x, l, k_mem, v_mem, row(g_x_pre[l]), w_cq, w_co, row(g_x_post[l]),
                       row(g_ff_pre[l]), w_ff1, w_ff2, row(g_ff_post[l]))
    return x
```

```python
import functools
import math

import jax
import jax.numpy as jnp
import numpy as np
from jax import lax
from jax.experimental import pallas as pl
from jax.experimental.pallas import tpu as pltpu

F32 = jnp.float32
BF16 = jnp.bfloat16

D_MODEL = 1024
N_MEM = 256
N_BRANCH = 4
W_BRANCH = D_MODEL // 2
POOL_WINDOWS = (2, 4, 8, 16)
POOL_GROUP = W_BRANCH // len(POOL_WINDOWS)
DIL_GROUPS = ((128, 1), (512, 4), (2048, 16))
ATT_HEADS = 8
ATT_HEAD_DIM = W_BRANCH // ATT_HEADS
SSM_GROUP = 16
SSM_GROUPS = W_BRANCH // SSM_GROUP
SSM_STATE = 64
SSM_STATES = SSM_GROUPS * SSM_STATE
SGU_CHUNK = 128
SGU_GROUPS = 4
X_HEADS = 4
X_HEAD_DIM = 128
D_FF = 4 * D_MODEL
REL_BUCKETS = 32
REL_MAX_DIST = 2048
EPS = 1e-6
NEG_INF = -1e30
N_ATT_COLS = 3 * len(DIL_GROUPS) * W_BRANCH
OFF_GATE = W_BRANCH + N_ATT_COLS + W_BRANCH + 2 * W_BRANCH

V7X_LANES = 128
V7X_SUBLANES = 8
V7X_MXU_DIM = 256
V7X_VMEM_LIMIT_BYTES = 56 * 1024 * 1024

TOKEN_BLOCK = 1024
POOL_HALO = 16
ATT_BAND = 128
ATT_SPAN = 512
ATT_UNITS = 4
IN_PROJ_SLAB_SETS = 3
LOG2E = 1.4426950408889634
ATT_Q_SCALE = ATT_HEAD_DIM ** -0.5 * LOG2E
S5_STEPS = 64
S5_PITCH = S5_STEPS + 8
S5_LANE_CHUNK = 512
FF_CHUNK = 512


def _params(*sem):
    return pltpu.CompilerParams(dimension_semantics=sem, vmem_limit_bytes=V7X_VMEM_LIMIT_BYTES)


def _const_spec(shape):
    nd = len(shape)
    return pl.BlockSpec(shape, lambda *_: (0,) * nd, pipeline_mode=pl.Buffered(1))


def _layer_spec(shape, l):
    nd = len(shape) - 1
    return pl.BlockSpec((None,) + tuple(shape[1:]), lambda *_: (l,) + (0,) * nd, pipeline_mode=pl.Buffered(1))


def _rms(xf, g):
    return xf * lax.rsqrt(jnp.mean(xf * xf, axis=-1, keepdims=True) + EPS) * g


def _gelu(x):
    c = math.sqrt(2.0 / math.pi)
    return 0.5 * x * (1.0 + jnp.tanh(c * (x + 0.044715 * (x * x * x))))


def _sigmoid(x):
    return 1.0 / (1.0 + jnp.exp(-x))


def _dot(a, b):
    return jnp.dot(a, b, preferred_element_type=F32)


def _in_proj_kernel(x_ref, xh_ref, g_ref, w_ref, poolw_ref, pscale_ref, lng_ref, lnb_ref, ws_ref, bst_ref,
                    a_ref, qkv1_ref, qkv2_ref, qkv3_ref, ssm_ref, d_ref, slab_ref, ext_ref):
    h = _rms(x_ref[...], g_ref[...]).astype(BF16)
    tm = x_ref.shape[0]
    step = W_BRANCH
    ng = len(DIL_GROUPS)
    nslab = step // V7X_LANES
    i = pl.program_id(1)
    proj = lambda lo: _dot(h, w_ref[:, lo:lo + step])

    lo_sgu = W_BRANCH + N_ATT_COLS + W_BRANCH
    u = _gelu(proj(lo_sgu))
    v = _gelu(proj(lo_sgu + step))
    vc = v - jnp.mean(v, axis=-1, keepdims=True)
    var = jnp.mean(vc * vc, axis=-1, keepdims=True)
    vn = (vc * lax.rsqrt(var + EPS) * lng_ref[...] + lnb_ref[...]).astype(BF16)

    halo = _dot(_rms(xh_ref[...], g_ref[...]).astype(BF16), w_ref[:, 0:step])
    ext_ref[0:POOL_HALO, :] = jnp.where(i == 0, 0.0, halo)
    ext_ref[POOL_HALO:, :] = proj(0)

    def qkv(which):
        dst = slice(which * step, (which + 1) * step)
        for g, out_ref in enumerate((qkv1_ref, qkv2_ref, qkv3_ref)):
            y = proj(W_BRANCH + (which * ng + g) * step)
            if which == 0:
                y = y * ATT_Q_SCALE
            dil = DIL_GROUPS[g][1]
            if dil == 1:
                out_ref[:, dst] = y.astype(BF16)
                continue
            base = ((which * (ng - 1) + g - 1) % IN_PROJ_SLAB_SETS) * nslab
            for c in range(nslab):
                slab_ref[base + c] = y[:, c * V7X_LANES:(c + 1) * V7X_LANES]
            for c in range(nslab):
                cols = slice(which * step + c * V7X_LANES, which * step + (c + 1) * V7X_LANES)
                for r in range(dil):
                    out_ref[r, :, cols] = slab_ref[base + c, pl.ds(r, tm // dil, stride=dil), :].astype(BF16)

    qkv(0)

    tpos = i * tm + lax.broadcasted_iota(jnp.int32, (tm, 1), 0)
    for gi, win in enumerate(POOL_WINDOWS):
        lanes = slice(gi * POOL_GROUP, (gi + 1) * POOL_GROUP)
        tok = ext_ref[POOL_HALO:POOL_HALO + tm, lanes]
        s = tok
        for j in range(1, win):
            s = s + ext_ref[POOL_HALO - j:POOL_HALO - j + tm, lanes]
        cnt = jnp.minimum(tpos + 1, win).astype(F32)
        p = s / cnt - tok
        a_ref[:, lanes] = (_dot(p.astype(BF16), poolw_ref[gi]) * pscale_ref[:, lanes]).astype(BF16)

    qkv(1)

    T = SGU_CHUNK
    gd = W_BRANCH // SGU_GROUPS
    causal = lax.broadcasted_iota(jnp.int32, (T, T), 0) >= lax.broadcasted_iota(jnp.int32, (T, T), 1)
    for g in range(SGU_GROUPS):
        wg = jnp.where(causal, ws_ref[g], 0.0).astype(BF16)
        bias = bst_ref[:, g:g + 1]
        cols = slice(g * gd, (g + 1) * gd)
        for c in range(tm // T):
            rows = slice(c * T, (c + 1) * T)
            sv = _dot(wg, vn[rows, cols]) + bias
            d_ref[rows, cols] = (u[rows, cols] * sv).astype(BF16)

    qkv(2)
    ssm_ref[...] = proj(W_BRANCH + N_ATT_COLS).astype(BF16)


def _residue_major_shape(B, S, dil, width):
    return (B, S // ATT_SPAN, dil, ATT_SPAN // dil, width)


def _in_proj(x, l, g, w, pool_w, pool_scale, ln_g, ln_b, w_s, b_s_t):
    B, S, D = x.shape
    tm = ATT_SPAN
    qw = 3 * W_BRANCH
    d2, d3 = DIL_GROUPS[1][1], DIL_GROUPS[2][1]
    tok = lambda width: pl.BlockSpec((None, tm, width), lambda b, i: (b, i, 0))
    return pl.pallas_call(
        _in_proj_kernel,
        grid=(B, S // tm),
        in_specs=[
            tok(D),
            pl.BlockSpec((None, POOL_HALO, D), lambda b, i: (b, jnp.maximum(i * (tm // POOL_HALO) - 1, 0), 0)),
            _const_spec((1, D)),
            _layer_spec(w.shape, l),
            _const_spec(pool_w.shape), _const_spec(pool_scale.shape), _const_spec(ln_g.shape),
            _const_spec(ln_b.shape), _const_spec(w_s.shape), _const_spec(b_s_t.shape),
        ],
        out_specs=[
            tok(W_BRANCH),
            tok(qw),
            pl.BlockSpec((None, None, d2, tm // d2, qw), lambda b, i: (b, i, 0, 0, 0)),
            pl.BlockSpec((None, None, d3, tm // d3, qw), lambda b, i: (b, i, 0, 0, 0)),
            tok(W_BRANCH),
            tok(W_BRANCH),
        ],
        out_shape=[
            jax.ShapeDtypeStruct((B, S, W_BRANCH), BF16),
            jax.ShapeDtypeStruct((B, S, qw), BF16),
            jax.ShapeDtypeStruct(_residue_major_shape(B, S, d2, qw), BF16),
            jax.ShapeDtypeStruct(_residue_major_shape(B, S, d3, qw), BF16),
            jax.ShapeDtypeStruct((B, S, W_BRANCH), BF16),
            jax.ShapeDtypeStruct((B, S, W_BRANCH), BF16),
        ],
        scratch_shapes=[
            pltpu.VMEM((IN_PROJ_SLAB_SETS * W_BRANCH // V7X_LANES, tm, V7X_LANES), F32),
            pltpu.VMEM((POOL_HALO + tm, W_BRANCH), F32),
        ],
        compiler_params=_params("parallel", "parallel"),
        name="in_proj",
    )(x, x, g, w, pool_w, pool_scale, ln_g, ln_b, w_s, b_s_t)


def _attn_tasks(layout, first_step, q_ref, kp_ref, kc_ref, vp_ref, vc_ref, bias_ref, ind_ref, o_ref, st_ref):
    band, W = ATT_BAND, W_BRANCH
    lane = lax.broadcasted_iota(jnp.int32, (1, V7X_LANES), 1)
    low = lane < ATT_HEAD_DIM
    cache = {}

    def unit(u):
        if u in cache:
            return cache[u]
        table = jnp.where(first_step, 1, 0)
        if layout == "natural":
            if "ext" not in cache:
                cache["ext"] = (jnp.concatenate([kp_ref[...], kc_ref[...]], axis=0),
                                jnp.concatenate([vp_ref[...], vc_ref[...]], axis=0))
            k_ext, v_ext = cache["ext"]
            rows = slice(u * band, (u + 1) * band)
            cache[u] = (q_ref[rows, :], k_ext[u * band:(u + 2) * band], v_ext[u * band:(u + 2) * band],
                        table if u == 0 else 0,
                        lambda cols, val: o_ref.__setitem__((rows, cols), val),
                        lambda val: st_ref.__setitem__((rows, slice(None)), val))
        elif layout == "span":
            cat = lambda p, c: jnp.concatenate([p[u], c[u]], axis=0)
            cache[u] = (q_ref[u], cat(kp_ref, kc_ref), cat(vp_ref, vc_ref), table,
                        lambda cols, val: o_ref.__setitem__((u, slice(None), cols), val),
                        lambda val: st_ref.__setitem__((u,), val))
        else:
            sp, per = q_ref.shape[0], q_ref.shape[2]
            get = lambda ref: ref[:, u].reshape(band, W)
            cat = lambda p, c: jnp.concatenate([get(p), get(c)], axis=0)
            cache[u] = (get(q_ref), cat(kp_ref, kc_ref), cat(vp_ref, vc_ref), table,
                        lambda cols, val: o_ref.__setitem__((slice(None), u, slice(None), cols),
                                                            val.reshape(sp, per, V7X_LANES)),
                        lambda val: st_ref.__setitem__((slice(None), u), val.reshape(sp, per, V7X_LANES)))
        return cache[u]

    def score(u, j):
        q, k, _, table, _, _ = unit(u)
        cols = slice(j * V7X_LANES, (j + 1) * V7X_LANES)
        qp = q[:, cols]
        zq = jnp.zeros_like(qp)
        q2 = jnp.concatenate([jnp.where(low, qp, zq), jnp.where(low, zq, qp)], axis=0)
        s = lax.dot_general(q2, k[:, cols], (((1,), (1,)), ((), ())), preferred_element_type=F32)
        return s + jnp.concatenate([bias_ref[table, 2 * j], bias_ref[table, 2 * j + 1]], axis=0)

    todo = [(u, j) for u in range(ATT_UNITS) for j in range(ATT_HEADS // 2)]
    live = {}

    def task(idx):
        def run():
            u, j = todo[idx]
            _, _, v, _, put_o, put_st = unit(u)
            s = live.pop("score") if "score" in live else score(u, j)
            if idx + 1 < len(todo):
                live["score"] = score(*todo[idx + 1])
            stats = jnp.zeros((band, V7X_LANES), F32) if j == 0 else live.pop("stats")
            cols = slice(j * V7X_LANES, (j + 1) * V7X_LANES)
            m = jnp.max(s, axis=-1, keepdims=True)
            pb = jnp.exp2(s - m).astype(BF16)
            p2 = jnp.concatenate([pb[:band], pb[band:]], axis=1)
            vp = v[:, cols]
            zv = jnp.zeros_like(vp)
            v2 = jnp.concatenate([jnp.where(low, vp, zv), jnp.where(low, zv, vp)], axis=0)
            pv = _dot(p2, jnp.concatenate([v2, ind_ref[j]], axis=1))
            put_o(cols, pv[:, :V7X_LANES].astype(BF16))
            stats = stats + pv[:, V7X_LANES:]
            for e in range(2):
                stats = jnp.where(lane == 2 * j + e, m[e * band:(e + 1) * band], stats)
            if j == ATT_HEADS // 2 - 1:
                put_st(stats)
            else:
                live["stats"] = stats
        return run

    return [task(idx) for idx in range(len(todo))]


def _attn_specs(qkv, dil, n_items):
    B = qkv.shape[0]
    W = W_BRANCH
    U = ATT_UNITS
    item = lambda g: jnp.minimum(g, n_items - 1)
    if dil == 1:
        nspan = qkv.shape[1] // (U * ATT_BAND)
        layout = "natural"
        blk = lambda width: (None, U * ATT_BAND, width)
        prev_blk = (None, ATT_BAND, W)
        pos = lambda g: (item(g) // nspan, item(g) % nspan)
        first = lambda g: pos(g)[1] == 0
        idx = lambda which: (lambda g: (*pos(g), which))
        idx_prev = lambda which: (lambda g: (pos(g)[0], jnp.maximum(U * pos(g)[1] - 1, 0), which))
        out_shape = lambda width: (B, qkv.shape[1], width)
    else:
        nspan, _, per, _ = qkv.shape[1:]
        spans = ATT_BAND // per
        rgroups = dil // U
        nband = nspan // spans
        layout = "span" if spans == 1 else "band"
        blk = lambda width: ((None, None, U, per, width) if spans == 1 else (None, spans, U, per, width))
        prev_blk = blk(W)
        pos = lambda g: (item(g) // (nband * rgroups), (item(g) // rgroups) % nband, item(g) % rgroups)
        first = lambda g: pos(g)[1] == 0
        idx = lambda which: (lambda g: (*pos(g), 0, which))
        idx_prev = lambda which: (lambda g: (pos(g)[0], jnp.maximum(pos(g)[1] - 1, 0), pos(g)[2], 0, which))
        out_shape = lambda width: (B, nspan, dil, per, width)
        assert B * nband * rgroups == n_items
    in_specs = [pl.BlockSpec(blk(W), idx(0)), pl.BlockSpec(prev_blk, idx_prev(1)), pl.BlockSpec(blk(W), idx(1)),
                pl.BlockSpec(prev_blk, idx_prev(2)), pl.BlockSpec(blk(W), idx(2))]
    return layout, first, in_specs, (lambda width: pl.BlockSpec(blk(width), idx(0))), out_shape


def _attn_kernel(layout, first, *refs):
    for task in _attn_tasks(layout, first(pl.program_id(0)), *refs):
        task()


def _attn_group(qkv, bias, dil):
    n_items = qkv.shape[0] * (qkv.shape[1] if qkv.ndim == 5 else qkv.shape[1] // ATT_SPAN)
    key_head = np.arange(4 * ATT_BAND)[:, None] // (2 * ATT_BAND)
    ind = jnp.asarray(np.stack([np.arange(V7X_LANES)[None, :] == ATT_HEADS + 2 * j + key_head
                                for j in range(ATT_HEADS // 2)]), BF16)
    layout, first, in_specs, out_spec, out_shape = _attn_specs(qkv, dil, n_items)
    return pl.pallas_call(
        functools.partial(_attn_kernel, layout, first),
        grid=(n_items,),
        in_specs=in_specs + [_const_spec(bias.shape), _const_spec(ind.shape)],
        out_specs=[out_spec(W_BRANCH), out_spec(V7X_LANES)],
        out_shape=[jax.ShapeDtypeStruct(out_shape(W_BRANCH), BF16),
                   jax.ShapeDtypeStruct(out_shape(V7X_LANES), F32)],
        compiler_params=_params("parallel"),
        name=f"attn_d{dil}",
    )(qkv, qkv, qkv, qkv, qkv, bias, ind)


def _attn_mix(o1_ref, o2_ref, o3_ref, s1_ref, s2_ref, s3_ref, out_ref, on_ref, sn_ref):
    lane = lax.broadcasted_iota(jnp.int32, (1, V7X_LANES), 1)
    low = lane < ATT_HEAD_DIM
    nslab = W_BRANCH // V7X_LANES
    for gi, (o_ref, s_ref) in enumerate(((o2_ref, s2_ref), (o3_ref, s3_ref))):
        dil, per = o_ref.shape[0], o_ref.shape[1]
        for r in range(dil):
            sn_ref[gi, pl.ds(r, per, stride=dil), :] = s_ref[r]
            for c in range(nslab):
                on_ref[gi * nslab + c, pl.ds(r, per, stride=dil), :] = (
                    o_ref[r, :, c * V7X_LANES:(c + 1) * V7X_LANES].astype(F32))
    stats = (s1_ref[...], sn_ref[0], sn_ref[1])
    for j in range(ATT_HEADS // 2):
        cols = slice(j * V7X_LANES, (j + 1) * V7X_LANES)
        wts = []
        for e in range(2):
            h = 2 * j + e
            ms = [x[:, h:h + 1] for x in stats]
            ls = [x[:, ATT_HEADS + h:ATT_HEADS + h + 1] for x in stats]
            top = jnp.maximum(jnp.maximum(ms[0], ms[1]), ms[2])
            ex = [jnp.exp2(x - top) for x in ms]
            inv = 1.0 / (ex[0] * ls[0] + ex[1] * ls[1] + ex[2] * ls[2])
            wts.append([x * inv for x in ex])
        vals = (o1_ref[:, cols].astype(F32), on_ref[j], on_ref[nslab + j])
        acc = None
        for gi in range(3):
            term = jnp.where(low, wts[0][gi], wts[1][gi]) * vals[gi]
            acc = term if acc is None else acc + term
        out_ref[:, cols] = acc.astype(BF16)


def _s5_kernel(*refs):
    n_io = 11
    h_ref, st_even, st_odd, utb_even, utb_odd = refs[n_io:n_io + 5]
    io, shared = refs[:n_io], refs[n_io + 5:]
    g = pl.program_id(0)

    @pl.when(g == 0)
    def _():
        h_ref[...] = jnp.zeros_like(h_ref)
        for ref in (st_even, st_odd, utb_even, utb_odd):
            ref[...] = jnp.zeros_like(ref)

    @pl.when(lax.rem(g, 2) == 0)
    def _():
        _s5_step(h_ref, st_even, st_odd, utb_even, *io, *shared)

    @pl.when(lax.rem(g, 2) == 1)
    def _():
        _s5_step(h_ref, st_odd, st_even, utb_odd, *io, *shared)


def _s5_step(h_ref, st_ac, st_b, utb_ac, u_ref, wbr_ref, wbi_ref, ar_ref, ai_ref, cr_ref, ci_ref, dsk_ref,
             wglu_ref, bglu_ref, out_ref, uslab_ref, oslab_ref, g_ref):
    nst = SSM_STATES
    nb, tc, W = u_ref.shape
    pitch = S5_PITCH
    lanes = V7X_LANES
    wslab = W // lanes
    tile = V7X_MXU_DIM
    chan_tile_states = tile * SSM_STATE // SSM_GROUP

    def stage_a_relayout():
        for s in range(wslab):
            for b in range(nb):
                uslab_ref[s, b * pitch:b * pitch + tc, :] = u_ref[b, :, s * lanes:(s + 1) * lanes].astype(F32)
        for s in range(wslab):
            for t in range(0, tc, 2):
                pair = jnp.concatenate([uslab_ref[s, pl.ds(t, nb, stride=pitch), :],
                                        uslab_ref[s, pl.ds(t + 1, nb, stride=pitch), :]], axis=0)
                utb_ac[t * nb:(t + 2) * nb, s * lanes:(s + 1) * lanes] = pair.astype(BF16)

    pieces = []

    def stage_c(n):
        def run():
            s0 = n * chan_tile_states
            cols = slice(n * tile, (n + 1) * tile)
            yr = _dot(st_ac[:, s0:s0 + chan_tile_states].astype(BF16), cr_ref[n])
            yi = _dot(st_ac[:, nst + s0:nst + s0 + chan_tile_states].astype(BF16), ci_ref[n])
            y = yr - yi + utb_ac[:, cols].astype(F32) * dsk_ref[:, cols]
            g_ref[:, cols] = _gelu(y)
        return run

    def stage_c_out():
        z = _dot(g_ref[...].astype(BF16), wglu_ref[...]) + bglu_ref[...]
        g_ref[...] = g_ref[...] * _sigmoid(z)
        for s in range(wslab):
            for t in range(tc):
                oslab_ref[s, pl.ds(t, nb, stride=pitch), :] = g_ref[t * nb:(t + 1) * nb, s * lanes:(s + 1) * lanes]
        for s in range(wslab):
            for b in range(nb):
                out_ref[b, :, s * lanes:(s + 1) * lanes] = oslab_ref[s, b * pitch:b * pitch + tc, :].astype(BF16)

    def stage_a(j, w_ref, off):
        def run():
            kt = (j * tile // chan_tile_states) * tile
            st_ac[:, off + j * tile:off + (j + 1) * tile] = _dot(utb_ac[:, kt:kt + tile], w_ref[j])
        return run

    pieces += [stage_c(n) for n in range(W // tile)] + [stage_c_out, stage_a_relayout]
    for j in range(nst // tile):
        pieces += [stage_a(j, wbr_ref, 0), stage_a(j, wbi_ref, nst)]

    cw = S5_LANE_CHUNK
    nchunk = nst // cw
    every = (nchunk * tc) // len(pieces)
    slot = 0
    for c in range(nchunk):
        lo = c * cw
        ar = jnp.broadcast_to(ar_ref[:, lo:lo + cw], (nb, cw))
        ai = jnp.broadcast_to(ai_ref[:, lo:lo + cw], (nb, cw))
        hr = h_ref[:, lo:lo + cw]
        hi = h_ref[:, nst + lo:nst + lo + cw]
        for t in range(tc):
            r = slice(t * nb, (t + 1) * nb)
            br = st_b[r, lo:lo + cw]
            bi = st_b[r, nst + lo:nst + lo + cw]
            hr, hi = ar * hr - ai * hi + br, ar * hi + ai * hr + bi
            st_b[r, lo:lo + cw] = hr
            st_b[r, nst + lo:nst + lo + cw] = hi
            slot += 1
            if slot % every == 0 and pieces:
                pieces.pop(0)()
        h_ref[:, lo:lo + cw] = hr
        h_ref[:, nst + lo:nst + lo + cw] = hi
    for piece in pieces:
        piece()


def _s5(u, l, wbr, wbi, ar, ai, cr, ci, dskip, wglu, bglu):
    B, S, W = u.shape
    tc = S5_STEPS
    n = S // tc
    return pl.pallas_call(
        _s5_kernel,
        grid=(n + 2,),
        in_specs=[pl.BlockSpec((B, tc, W), lambda g: (0, jnp.minimum(g, n - 1), 0))]
        + [_const_spec(a.shape) for a in (wbr, wbi, ar, ai, cr, ci, dskip)]
        + [_layer_spec(wglu.shape, l), _const_spec(bglu.shape)],
        out_specs=pl.BlockSpec((B, tc, W), lambda g: (0, jnp.maximum(g - 2, 0), 0)),
        out_shape=jax.ShapeDtypeStruct((B, S, W), BF16),
        scratch_shapes=[
            pltpu.VMEM((B, 2 * SSM_STATES), F32),
            pltpu.VMEM((B * tc, 2 * SSM_STATES), F32),
            pltpu.VMEM((B * tc, 2 * SSM_STATES), F32),
            pltpu.VMEM((B * tc, W), BF16),
            pltpu.VMEM((B * tc, W), BF16),
            pltpu.VMEM((W // V7X_LANES, B * S5_PITCH, V7X_LANES), F32),
            pltpu.VMEM((W // V7X_LANES, B * S5_PITCH, V7X_LANES), F32),
            pltpu.VMEM((B * tc, W), F32),
        ],
        compiler_params=_params("arbitrary"),
        name="s5",
    )(u, wbr, wbi, ar, ai, cr, ci, dskip, wglu, bglu)


def _s5_params(a_re, a_im, log_dt, b_re, b_im, c_re, c_im):
    G, P, C = SSM_GROUPS, SSM_STATE, SSM_GROUP
    lam_re = jnp.minimum(a_re, -1e-4)
    lam_im = a_im
    dt = jnp.exp(log_dt)[:, None]
    mag = jnp.exp(lam_re * dt)
    ab_re, ab_im = mag * jnp.cos(lam_im * dt), mag * jnp.sin(lam_im * dt)
    den = lam_re * lam_re + lam_im * lam_im
    f_re = ((ab_re - 1.0) * lam_re + ab_im * lam_im) / den
    f_im = (ab_im * lam_re - (ab_re - 1.0) * lam_im) / den
    bb_re = f_re[..., None] * b_re - f_im[..., None] * b_im
    bb_im = f_re[..., None] * b_im + f_im[..., None] * b_re
    tile = V7X_MXU_DIM
    gs, gc = tile // P, tile // C
    nts = G // gs
    sel = np.zeros((nts, gs, gc), np.float32)
    for j in range(nts):
        for n in range(gs):
            sel[j, n, (j * gs + n) % gc] = 1.0
    eye = np.eye(gc, dtype=np.float32)

    def in_mat(bb):
        t = jnp.einsum('jnpc,jnk->jkcnp', bb.reshape(nts, gs, P, C), sel)
        return t.reshape(nts, tile, tile).astype(BF16)

    def out_mat(cc):
        t = jnp.einsum('ngcp,gh->ngphc', cc.reshape(G // gc, gc, C, P), eye)
        return t.reshape(G // gc, gc * P, tile).astype(BF16)

    return (in_mat(bb_re), in_mat(bb_im), ab_re.reshape(1, G * P), ab_im.reshape(1, G * P),
            out_mat(c_re), out_mat(c_im))


def _merge_kernel(x_ref, a_ref, o1_ref, o2_ref, o3_ref, s1_ref, s2_ref, s3_ref, c_ref, d_ref,
                  gpre_ref, wg_ref, gb_ref, wup_ref, wout_ref, gpost_ref, out_ref, b_ref, on_ref, sn_ref):
    _attn_mix(o1_ref, o2_ref, o3_ref, s1_ref, s2_ref, s3_ref, b_ref, on_ref, sn_ref)
    x = x_ref[...]
    h = _rms(x, gpre_ref[...]).astype(BF16)
    merged = None
    for i, br in enumerate((a_ref, b_ref, c_ref, d_ref)):
        gate = _sigmoid(_dot(h, wg_ref[:, i * D_MODEL:(i + 1) * D_MODEL]) + gb_ref[i:i + 1, :])
        term = gate * _dot(br[...], wup_ref[i])
        merged = term if merged is None else merged + term
    y = _dot(merged.astype(BF16), wout_ref[...])
    out_ref[...] = x + _rms(y, gpost_ref[...])


def _merge(x, l, a, outs, stats, c, d, gpre, wg, gb, wup, wout, gpost):
    B, S, D = x.shape
    tm = ATT_SPAN
    W = W_BRANCH

    def tok(a_):
        if a_.ndim == 3:
            return pl.BlockSpec((None, tm, a_.shape[-1]), lambda b_, i: (b_, i, 0))
        return pl.BlockSpec((None, None) + a_.shape[2:], lambda b_, i: (b_, i, 0, 0, 0))

    return pl.pallas_call(
        _merge_kernel,
        grid=(B, S // tm),
        in_specs=[tok(a_) for a_ in (x, a, *outs, *stats, c, d)] + [
            _const_spec((1, D)), _layer_spec(wg.shape, l), _const_spec(gb.shape), _layer_spec(wup.shape, l),
            _layer_spec(wout.shape, l), _const_spec((1, D)),
        ],
        out_specs=tok(x),
        out_shape=jax.ShapeDtypeStruct((B, S, D), F32),
        scratch_shapes=[
            pltpu.VMEM((tm, W), BF16),
            pltpu.VMEM((2 * W // V7X_LANES, tm, V7X_LANES), F32),
            pltpu.VMEM((2, tm, V7X_LANES), F32),
        ],
        compiler_params=_params("parallel", "parallel"),
        name="merge",
    )(x, a, *outs, *stats, c, d, gpre, wg, gb, wup, wout, gpost)


def _mem_kv_kernel(mem_ref, g_ref, w_ref, k_ref, v_ref):
    mn = _rms(mem_ref[...], g_ref[...]).astype(BF16)
    hw = X_HEADS * X_HEAD_DIM
    k_ref[...] = _dot(mn, w_ref[:, :hw]).astype(BF16)
    v_ref[...] = _dot(mn, w_ref[:, hw:]).astype(BF16)


def _mem_kv(mem, l, g, w):
    B, M, D = mem.shape
    hw = X_HEADS * X_HEAD_DIM
    ob = pl.BlockSpec((None, M, hw), lambda b: (b, 0, 0))
    return pl.pallas_call(
        _mem_kv_kernel,
        grid=(B,),
        in_specs=[pl.BlockSpec((None, M, D), lambda b: (b, 0, 0)), _const_spec((1, D)), _layer_spec(w.shape, l)],
        out_specs=[ob, ob],
        out_shape=[jax.ShapeDtypeStruct((B, M, hw), BF16)] * 2,
        compiler_params=_params("parallel"),
        name="mem_kv",
    )(mem, g, w)


def _xattn_ffn_kernel(x_ref, k_ref, v_ref, gxpre_ref, wq_ref, wo_ref, gxpost_ref, gfpre_ref, w1_ref, w2_ref,
                      gfpost_ref, out_ref, o_ref, acc_ref):
    x = x_ref[...]
    h = _rms(x, gxpre_ref[...]).astype(BF16)
    q = (_dot(h, wq_ref[...]) * (X_HEAD_DIM ** -0.5)).astype(BF16)
    heads = [slice(hd * X_HEAD_DIM, (hd + 1) * X_HEAD_DIM) for hd in range(X_HEADS)]
    scores = [lax.dot_general(q[:, cols], k_ref[:, cols], (((1,), (1,)), ((), ())), preferred_element_type=F32)
              for cols in heads]
    for cols, s in zip(heads, scores):
        m = jnp.max(s, axis=-1, keepdims=True)
        p = jnp.exp(s - m)
        l = jnp.sum(p, axis=-1, keepdims=True)
        o_ref[:, cols] = (_dot(p.astype(BF16), v_ref[:, cols]) * (1.0 / l)).astype(BF16)
    x = x + _rms(_dot(o_ref[...], wo_ref[...]), gxpost_ref[...])
    h = _rms(x, gfpre_ref[...]).astype(BF16)
    for c in range(D_FF // FF_CHUNK):
        cols = slice(c * FF_CHUNK, (c + 1) * FF_CHUNK)
        a = jnp.maximum(_dot(h, w1_ref[:, cols]), 0.0)
        a = (a * a).astype(BF16)
        part = _dot(a, w2_ref[cols, :])
        if c == 0:
            acc_ref[...] = part
        else:
            acc_ref[...] += part
    out_ref[...] = x + _rms(acc_ref[...], gfpost_ref[...])


def _xattn_ffn(x, l, k, v, gxpre, wq, wo, gxpost, gfpre, w1, w2, gfpost):
    B, S, D = x.shape
    tm = TOKEN_BLOCK
    hw = X_HEADS * X_HEAD_DIM
    xb = pl.BlockSpec((None, tm, D), lambda b, i: (b, i, 0))
    kb = pl.BlockSpec((None, N_MEM, hw), lambda b, i: (b, 0, 0))
    consts = (gxpre, wq, wo, gxpost, gfpre, w1, w2, gfpost)
    return pl.pallas_call(
        _xattn_ffn_kernel,
        grid=(B, S // tm),
        in_specs=[xb, kb, kb] + [_layer_spec(a.shape, l) if a.ndim == 3 else _const_spec(a.shape) for a in consts],
        out_specs=xb,
        out_shape=jax.ShapeDtypeStruct((B, S, D), F32),
        scratch_shapes=[pltpu.VMEM((tm, hw), BF16), pltpu.VMEM((tm, D), F32)],
        compiler_params=_params("parallel", "parallel"),
        name="xattn_ffn",
    )(x, k, v, *consts)


def _t5_bucket(n):
    exact = REL_BUCKETS // 2
    nf = np.maximum(n, 1).astype(np.float32)
    large = exact + (np.log(nf / exact) / np.log(REL_MAX_DIST / exact) * (REL_BUCKETS - exact)).astype(np.int32)
    large = np.minimum(large, REL_BUCKETS - 1)
    return np.where(n < exact, n, large).astype(np.int32)


def _band_bias(rel_bias, g, band, dil):
    d = np.arange(band + 1)
    table = rel_bias[jnp.asarray(_t5_bucket(d * dil))][:, g * ATT_HEADS:(g + 1) * ATT_HEADS].astype(F32) * LOG2E
    pad = jnp.full((ATT_HEADS, band - 1), NEG_INF, F32)
    f = jnp.concatenate([pad, table[::-1].T, pad], axis=1)
    full = jnp.stack([f[:, band - 1 - i:3 * band - 1 - i] for i in range(band)], axis=1)
    no_prev = jnp.concatenate([jnp.full((ATT_HEADS, band, band), NEG_INF, F32), full[:, :, band:]], axis=2)
    return jnp.stack([full, no_prev], axis=0)


def kernel(x, mem, rel_bias, g_mix_pre, g_mix_post, w_in, gate_b, pool_w, pool_scale, a_re, a_im, log_dt,
           b_re, b_im, c_re, c_im, d_skip, w_glu, b_glu, sgu_ln_g, sgu_ln_b, w_s, b_s, w_up, w_out,
           g_x_pre, g_x_post, g_mem, w_cq, w_ckv, w_co, g_ff_pre, g_ff_post, w_ff1, w_ff2):
    B, S, D = x.shape
    depth = w_in.shape[0]
    assert D == D_MODEL and S % (TOKEN_BLOCK) == 0 and B == V7X_SUBLANES
    for win, dil in DIL_GROUPS:
        assert win // dil == ATT_BAND and (S // dil) % ATT_BAND == 0

    biases = [_band_bias(rel_bias, g, win // dil, dil) for g, (win, dil) in enumerate(DIL_GROUPS)]
    row = lambda a: a.reshape(1, -1).astype(F32)
    w_mix, w_gate = w_in[:, :, :OFF_GATE].astype(BF16), w_in[:, :, OFF_GATE:].astype(BF16)
    w_up, w_out, w_glu, w_cq, w_ckv, w_co, w_ff1, w_ff2 = (
        a.astype(BF16) for a in (w_up, w_out, w_glu, w_cq, w_ckv, w_co, w_ff1, w_ff2))

    for l in range(depth):
        a_out, qkv1, qkv2, qkv3, ssm_in, d_out = _in_proj(
            x, l, row(g_mix_pre[l]), w_mix, pool_w[l].astype(BF16), row(pool_scale[l]),
            row(sgu_ln_g[l]), row(sgu_ln_b[l]), w_s[l], b_s[l].T)
        s5p = _s5_params(a_re[l], a_im[l], log_dt[l], b_re[l], b_im[l], c_re[l], c_im[l])
        c_out = _s5(ssm_in, l, *s5p, row(d_skip[l]), w_glu, row(b_glu[l]))
        outs, stats = zip(*[_attn_group(qkv, bias, dil)
                            for qkv, bias, (win, dil) in zip((qkv1, qkv2, qkv3), biases, DIL_GROUPS)])
        x = _merge(x, l, a_out, outs, stats, c_out, d_out, row(g_mix_pre[l]), w_gate, gate_b[l], w_up, w_out,
                   row(g_mix_post[l]))
        k_mem, v_mem = _mem_kv(mem, l, row(g_mem[l]), w_ckv)
        x = _xattn_ffn(x, l, k_mem, v_mem, row(g_x_pre[l]), w_cq, w_co, row(g_x_post[l]),
                       row(g_ff_pre[l]), w_ff1, w_ff2, row(g_ff_post[l]))
    return x
```

```python
import functools
import math

import jax
import jax.numpy as jnp
import numpy as np
from jax import lax
from jax.experimental import pallas as pl
from jax.experimental.pallas import tpu as pltpu

F32 = jnp.float32
BF16 = jnp.bfloat16

D_MODEL = 1024
N_MEM = 256
N_BRANCH = 4
W_BRANCH = D_MODEL // 2
POOL_WINDOWS = (2, 4, 8, 16)
POOL_GROUP = W_BRANCH // len(POOL_WINDOWS)
DIL_GROUPS = ((128, 1), (512, 4), (2048, 16))
ATT_HEADS = 8
ATT_HEAD_DIM = W_BRANCH // ATT_HEADS
SSM_GROUP = 16
SSM_GROUPS = W_BRANCH // SSM_GROUP
SSM_STATE = 64
SSM_STATES = SSM_GROUPS * SSM_STATE
SGU_CHUNK = 128
SGU_GROUPS = 4
X_HEADS = 4
X_HEAD_DIM = 128
D_FF = 4 * D_MODEL
REL_BUCKETS = 32
REL_MAX_DIST = 2048
EPS = 1e-6
NEG_INF = -1e30
N_ATT_COLS = 3 * len(DIL_GROUPS) * W_BRANCH
OFF_GATE = W_BRANCH + N_ATT_COLS + W_BRANCH + 2 * W_BRANCH

V7X_LANES = 128
V7X_SUBLANES = 8
V7X_MXU_DIM = 256
V7X_VMEM_LIMIT_BYTES = 56 * 1024 * 1024

TOKEN_BLOCK = 1024
POOL_HALO = 16
ATT_BAND = 128
ATT_SPAN = 512
ATT_UNITS = 16
IN_PROJ_SLAB_SETS = 3
LOG2E = 1.4426950408889634
ATT_Q_SCALE = ATT_HEAD_DIM ** -0.5 * LOG2E
S5_STEPS = 64
S5_PITCH = S5_STEPS + 8
S5_LANE_CHUNK = 512
FF_CHUNK = 512


def _params(*sem):
    return pltpu.CompilerParams(dimension_semantics=sem, vmem_limit_bytes=V7X_VMEM_LIMIT_BYTES)


def _const_spec(shape):
    nd = len(shape)
    return pl.BlockSpec(shape, lambda *_: (0,) * nd, pipeline_mode=pl.Buffered(1))


def _layer_spec(shape, l):
    nd = len(shape) - 1
    return pl.BlockSpec((None,) + tuple(shape[1:]), lambda *_: (l,) + (0,) * nd, pipeline_mode=pl.Buffered(1))


def _rms(xf, g):
    return xf * lax.rsqrt(jnp.mean(xf * xf, axis=-1, keepdims=True) + EPS) * g


def _gelu(x):
    c = math.sqrt(2.0 / math.pi)
    return 0.5 * x * (1.0 + jnp.tanh(c * (x + 0.044715 * (x * x * x))))


def _sigmoid(x):
    return 1.0 / (1.0 + jnp.exp(-x))


def _dot(a, b):
    return jnp.dot(a, b, preferred_element_type=F32)


def _in_proj_kernel(x_ref, xh_ref, g_ref, w_ref, poolw_ref, pscale_ref, lng_ref, lnb_ref, ws_ref, bst_ref,
                    a_ref, qkv1_ref, qkv2_ref, qkv3_ref, ssm_ref, d_ref, slab_ref, ext_ref):
    h = _rms(x_ref[...], g_ref[...]).astype(BF16)
    tm = x_ref.shape[0]
    step = W_BRANCH
    ng = len(DIL_GROUPS)
    nslab = step // V7X_LANES
    i = pl.program_id(1)
    proj = lambda lo: _dot(h, w_ref[:, lo:lo + step])

    lo_sgu = W_BRANCH + N_ATT_COLS + W_BRANCH
    u = _gelu(proj(lo_sgu))
    v = _gelu(proj(lo_sgu + step))
    vc = v - jnp.mean(v, axis=-1, keepdims=True)
    var = jnp.mean(vc * vc, axis=-1, keepdims=True)
    vn = (vc * lax.rsqrt(var + EPS) * lng_ref[...] + lnb_ref[...]).astype(BF16)

    halo = _dot(_rms(xh_ref[...], g_ref[...]).astype(BF16), w_ref[:, 0:step])
    ext_ref[0:POOL_HALO, :] = jnp.where(i == 0, 0.0, halo)
    ext_ref[POOL_HALO:, :] = proj(0)

    def qkv(which):
        dst = slice(which * step, (which + 1) * step)
        for g, out_ref in enumerate((qkv1_ref, qkv2_ref, qkv3_ref)):
            y = proj(W_BRANCH + (which * ng + g) * step)
            if which == 0:
                y = y * ATT_Q_SCALE
            dil = DIL_GROUPS[g][1]
            if dil == 1:
                out_ref[:, dst] = y.astype(BF16)
                continue
            base = ((which * (ng - 1) + g - 1) % IN_PROJ_SLAB_SETS) * nslab
            for c in range(nslab):
                slab_ref[base + c] = y[:, c * V7X_LANES:(c + 1) * V7X_LANES]
            for c in range(nslab):
                cols = slice(which * step + c * V7X_LANES, which * step + (c + 1) * V7X_LANES)
                for r in range(dil):
                    out_ref[r, :, cols] = slab_ref[base + c, pl.ds(r, tm // dil, stride=dil), :].astype(BF16)

    qkv(0)

    tpos = i * tm + lax.broadcasted_iota(jnp.int32, (tm, 1), 0)
    for gi, win in enumerate(POOL_WINDOWS):
        lanes = slice(gi * POOL_GROUP, (gi + 1) * POOL_GROUP)
        tok = ext_ref[POOL_HALO:POOL_HALO + tm, lanes]
        s = tok
        for j in range(1, win):
            s = s + ext_ref[POOL_HALO - j:POOL_HALO - j + tm, lanes]
        cnt = jnp.minimum(tpos + 1, win).astype(F32)
        p = s / cnt - tok
        a_ref[:, lanes] = (_dot(p.astype(BF16), poolw_ref[gi]) * pscale_ref[:, lanes]).astype(BF16)

    qkv(1)

    T = SGU_CHUNK
    gd = W_BRANCH // SGU_GROUPS
    causal = lax.broadcasted_iota(jnp.int32, (T, T), 0) >= lax.broadcasted_iota(jnp.int32, (T, T), 1)
    for g in range(SGU_GROUPS):
        wg = jnp.where(causal, ws_ref[g], 0.0).astype(BF16)
        bias = bst_ref[:, g:g + 1]
        cols = slice(g * gd, (g + 1) * gd)
        for c in range(tm // T):
            rows = slice(c * T, (c + 1) * T)
            sv = _dot(wg, vn[rows, cols]) + bias
            d_ref[rows, cols] = (u[rows, cols] * sv).astype(BF16)

    qkv(2)
    ssm_ref[...] = proj(W_BRANCH + N_ATT_COLS).astype(BF16)


def _residue_major_shape(B, S, dil, width):
    return (B, S // ATT_SPAN, dil, ATT_SPAN // dil, width)


def _in_proj(x, l, g, w, pool_w, pool_scale, ln_g, ln_b, w_s, b_s_t):
    B, S, D = x.shape
    tm = ATT_SPAN
    qw = 3 * W_BRANCH
    d2, d3 = DIL_GROUPS[1][1], DIL_GROUPS[2][1]
    tok = lambda width: pl.BlockSpec((None, tm, width), lambda b, i: (b, i, 0))
    return pl.pallas_call(
        _in_proj_kernel,
        grid=(B, S // tm),
        in_specs=[
            tok(D),
            pl.BlockSpec((None, POOL_HALO, D), lambda b, i: (b, jnp.maximum(i * (tm // POOL_HALO) - 1, 0), 0)),
            _const_spec((1, D)),
            _layer_spec(w.shape, l),
            _const_spec(pool_w.shape), _const_spec(pool_scale.shape), _const_spec(ln_g.shape),
            _const_spec(ln_b.shape), _const_spec(w_s.shape), _const_spec(b_s_t.shape),
        ],
        out_specs=[
            tok(W_BRANCH),
            tok(qw),
            pl.BlockSpec((None, None, d2, tm // d2, qw), lambda b, i: (b, i, 0, 0, 0)),
            pl.BlockSpec((None, None, d3, tm // d3, qw), lambda b, i: (b, i, 0, 0, 0)),
            tok(W_BRANCH),
            tok(W_BRANCH),
        ],
        out_shape=[
            jax.ShapeDtypeStruct((B, S, W_BRANCH), BF16),
            jax.ShapeDtypeStruct((B, S, qw), BF16),
            jax.ShapeDtypeStruct(_residue_major_shape(B, S, d2, qw), BF16),
            jax.ShapeDtypeStruct(_residue_major_shape(B, S, d3, qw), BF16),
            jax.ShapeDtypeStruct((B, S, W_BRANCH), BF16),
            jax.ShapeDtypeStruct((B, S, W_BRANCH), BF16),
        ],
        scratch_shapes=[
            pltpu.VMEM((IN_PROJ_SLAB_SETS * W_BRANCH // V7X_LANES, tm, V7X_LANES), F32),
            pltpu.VMEM((POOL_HALO + tm, W_BRANCH), F32),
        ],
        compiler_params=_params("parallel", "parallel"),
        name="in_proj",
    )(x, x, g, w, pool_w, pool_scale, ln_g, ln_b, w_s, b_s_t)


def _attn_tasks(layout, first_step, q_ref, kp_ref, kc_ref, vp_ref, vc_ref, bias_ref, ind_ref, o_ref, st_ref):
    band, W = ATT_BAND, W_BRANCH
    lane = lax.broadcasted_iota(jnp.int32, (1, V7X_LANES), 1)
    low = lane < ATT_HEAD_DIM
    cache = {}

    def unit(u):
        if u in cache:
            return cache[u]
        first_table = jnp.where(first_step, 1, 0)
        if layout == "natural":
            rows = slice(u * band, (u + 1) * band)
            if u == 0:
                k = jnp.concatenate([kp_ref[...], kc_ref[rows, :]], axis=0)
                v = jnp.concatenate([vp_ref[...], vc_ref[rows, :]], axis=0)
            else:
                k, v = kc_ref[(u - 1) * band:(u + 1) * band, :], vc_ref[(u - 1) * band:(u + 1) * band, :]
            cache[u] = (q_ref[rows, :], k, v, first_table if u == 0 else 0,
                        lambda cols, val: o_ref.__setitem__((rows, cols), val),
                        lambda val: st_ref.__setitem__((rows, slice(None)), val))
        else:
            per = q_ref.shape[2]
            spb = band // per
            bb, r = divmod(u, q_ref.shape[1])
            cur = slice(bb * spb, (bb + 1) * spb)
            get = lambda ref, sp: ref[sp, r].reshape(band, W)
            if bb == 0:
                prev = lambda p, c: get(p, slice(None))
            else:
                prev = lambda p, c: get(c, slice((bb - 1) * spb, bb * spb))
            cache[u] = (get(q_ref, cur),
                        jnp.concatenate([prev(kp_ref, kc_ref), get(kc_ref, cur)], axis=0),
                        jnp.concatenate([prev(vp_ref, vc_ref), get(vc_ref, cur)], axis=0),
                        first_table if bb == 0 else 0,
                        lambda cols, val: o_ref.__setitem__((cur, r, slice(None), cols),
                                                            val.reshape(spb, per, V7X_LANES)),
                        lambda val: st_ref.__setitem__((cur, r), val.reshape(spb, per, V7X_LANES)))
        return cache[u]

    def score(u, j):
        q, k, _, table, _, _ = unit(u)
        cols = slice(j * V7X_LANES, (j + 1) * V7X_LANES)
        qp = q[:, cols]
        zq = jnp.zeros_like(qp)
        q2 = jnp.concatenate([jnp.where(low, qp, zq), jnp.where(low, zq, qp)], axis=0)
        s = lax.dot_general(q2, k[:, cols], (((1,), (1,)), ((), ())), preferred_element_type=F32)
        return s + jnp.concatenate([bias_ref[table, 2 * j], bias_ref[table, 2 * j + 1]], axis=0)

    n_units = q_ref.shape[0] // band if layout == "natural" else q_ref.shape[0] * q_ref.shape[1] * q_ref.shape[2] // band
    todo = [(u, j) for u in range(n_units) for j in range(ATT_HEADS // 2)]
    live = {}

    def task(idx):
        def run():
            u, j = todo[idx]
            _, _, v, _, put_o, put_st = unit(u)
            s = live.pop("score") if "score" in live else score(u, j)
            if idx + 1 < len(todo):
                live["score"] = score(*todo[idx + 1])
            stats = jnp.zeros((band, V7X_LANES), F32) if j == 0 else live.pop("stats")
            cols = slice(j * V7X_LANES, (j + 1) * V7X_LANES)
            m = jnp.max(s, axis=-1, keepdims=True)
            pb = jnp.exp2(s - m).astype(BF16)
            p2 = jnp.concatenate([pb[:band], pb[band:]], axis=1)
            vp = v[:, cols]
            zv = jnp.zeros_like(vp)
            v2 = jnp.concatenate([jnp.where(low, vp, zv), jnp.where(low, zv, vp)], axis=0)
            pv = _dot(p2, jnp.concatenate([v2, ind_ref[j]], axis=1))
            put_o(cols, pv[:, :V7X_LANES].astype(BF16))
            stats = stats + pv[:, V7X_LANES:]
            for e in range(2):
                stats = jnp.where(lane == 2 * j + e, m[e * band:(e + 1) * band], stats)
            if j == ATT_HEADS // 2 - 1:
                put_st(stats)
            else:
                live["stats"] = stats
        return run

    return [task(idx) for idx in range(len(todo))]


def _attn_specs(qkv, dil, n_items):
    B = qkv.shape[0]
    W = W_BRANCH
    U = ATT_UNITS
    item = lambda g: jnp.minimum(g, n_items - 1)
    if dil == 1:
        nspan = qkv.shape[1] // (U * ATT_BAND)
        layout = "natural"
        blk = lambda width: (None, U * ATT_BAND, width)
        prev_blk = (None, ATT_BAND, W)
        pos = lambda g: (item(g) // nspan, item(g) % nspan)
        first = lambda g: pos(g)[1] == 0
        idx = lambda which: (lambda g: (*pos(g), which))
        idx_prev = lambda which: (lambda g: (pos(g)[0], jnp.maximum(U * pos(g)[1] - 1, 0), which))
        out_shape = lambda width: (B, qkv.shape[1], width)
    else:
        nspan, _, per, _ = qkv.shape[1:]
        spb = ATT_BAND // per
        res = min(dil, U)
        bands = U // res
        rgroups, nblk = dil // res, nspan // (spb * bands)
        layout = "residue-major"
        blk = lambda width: (None, spb * bands, res, per, width)
        prev_blk = (None, spb, res, per, W)
        pos = lambda g: (item(g) // (nblk * rgroups), (item(g) // rgroups) % nblk, item(g) % rgroups)
        first = lambda g: pos(g)[1] == 0
        idx = lambda which: (lambda g: (*pos(g), 0, which))
        idx_prev = lambda which: (lambda g: (pos(g)[0], jnp.maximum(pos(g)[1] * bands - 1, 0), pos(g)[2], 0, which))
        out_shape = lambda width: (B, nspan, dil, per, width)
        assert B * nblk * rgroups == n_items
    in_specs = [pl.BlockSpec(blk(W), idx(0)), pl.BlockSpec(prev_blk, idx_prev(1)), pl.BlockSpec(blk(W), idx(1)),
                pl.BlockSpec(prev_blk, idx_prev(2)), pl.BlockSpec(blk(W), idx(2))]
    return layout, first, in_specs, (lambda width: pl.BlockSpec(blk(width), idx(0))), out_shape


def _attn_kernel(layout, first, *refs):
    for task in _attn_tasks(layout, first(pl.program_id(0)), *refs):
        task()


def _attn_group(qkv, bias, dil):
    n_items = qkv.shape[0] * (qkv.shape[1] * qkv.shape[2] * qkv.shape[3] if qkv.ndim == 5 else qkv.shape[1]) // (ATT_UNITS * ATT_BAND)
    key_head = np.arange(4 * ATT_BAND)[:, None] // (2 * ATT_BAND)
    ind = jnp.asarray(np.stack([np.arange(V7X_LANES)[None, :] == ATT_HEADS + 2 * j + key_head
                                for j in range(ATT_HEADS // 2)]), BF16)
    layout, first, in_specs, out_spec, out_shape = _attn_specs(qkv, dil, n_items)
    return pl.pallas_call(
        functools.partial(_attn_kernel, layout, first),
        grid=(n_items,),
        in_specs=in_specs + [_const_spec(bias.shape), _const_spec(ind.shape)],
        out_specs=[out_spec(W_BRANCH), out_spec(V7X_LANES)],
        out_shape=[jax.ShapeDtypeStruct(out_shape(W_BRANCH), BF16),
                   jax.ShapeDtypeStruct(out_shape(V7X_LANES), F32)],
        compiler_params=_params("parallel"),
        name=f"attn_d{dil}",
    )(qkv, qkv, qkv, qkv, qkv, bias, ind)


def _attn_mix(o1_ref, o2_ref, o3_ref, s1_ref, s2_ref, s3_ref, out_ref, on_ref, sn_ref):
    lane = lax.broadcasted_iota(jnp.int32, (1, V7X_LANES), 1)
    low = lane < ATT_HEAD_DIM
    nslab = W_BRANCH // V7X_LANES
    for gi, (o_ref, s_ref) in enumerate(((o2_ref, s2_ref), (o3_ref, s3_ref))):
        dil, per = o_ref.shape[0], o_ref.shape[1]
        for r in range(dil):
            sn_ref[gi, pl.ds(r, per, stride=dil), :] = s_ref[r]
            for c in range(nslab):
                on_ref[gi * nslab + c, pl.ds(r, per, stride=dil), :] = (
                    o_ref[r, :, c * V7X_LANES:(c + 1) * V7X_LANES].astype(F32))
    stats = (s1_ref[...], sn_ref[0], sn_ref[1])
    for j in range(ATT_HEADS // 2):
        cols = slice(j * V7X_LANES, (j + 1) * V7X_LANES)
        wts = []
        for e in range(2):
            h = 2 * j + e
            ms = [x[:, h:h + 1] for x in stats]
            ls = [x[:, ATT_HEADS + h:ATT_HEADS + h + 1] for x in stats]
            top = jnp.maximum(jnp.maximum(ms[0], ms[1]), ms[2])
            ex = [jnp.exp2(x - top) for x in ms]
            inv = 1.0 / (ex[0] * ls[0] + ex[1] * ls[1] + ex[2] * ls[2])
            wts.append([x * inv for x in ex])
        vals = (o1_ref[:, cols].astype(F32), on_ref[j], on_ref[nslab + j])
        acc = None
        for gi in range(3):
            term = jnp.where(low, wts[0][gi], wts[1][gi]) * vals[gi]
            acc = term if acc is None else acc + term
        out_ref[:, cols] = acc.astype(BF16)


def _s5_kernel(*refs):
    n_io = 11
    h_ref, st_even, st_odd, utb_even, utb_odd = refs[n_io:n_io + 5]
    io, shared = refs[:n_io], refs[n_io + 5:]
    g = pl.program_id(0)

    @pl.when(g == 0)
    def _():
        h_ref[...] = jnp.zeros_like(h_ref)
        for ref in (st_even, st_odd, utb_even, utb_odd):
            ref[...] = jnp.zeros_like(ref)

    @pl.when(lax.rem(g, 2) == 0)
    def _():
        _s5_step(h_ref, st_even, st_odd, utb_even, *io, *shared)

    @pl.when(lax.rem(g, 2) == 1)
    def _():
        _s5_step(h_ref, st_odd, st_even, utb_odd, *io, *shared)


def _s5_step(h_ref, st_ac, st_b, utb_ac, u_ref, wbr_ref, wbi_ref, ar_ref, ai_ref, cr_ref, ci_ref, dsk_ref,
             wglu_ref, bglu_ref, out_ref, uslab_ref, oslab_ref, g_ref):
    nst = SSM_STATES
    nb, tc, W = u_ref.shape
    pitch = S5_PITCH
    lanes = V7X_LANES
    wslab = W // lanes
    tile = V7X_MXU_DIM
    chan_tile_states = tile * SSM_STATE // SSM_GROUP

    def stage_a_relayout():
        for s in range(wslab):
            for b in range(nb):
                uslab_ref[s, b * pitch:b * pitch + tc, :] = u_ref[b, :, s * lanes:(s + 1) * lanes].astype(F32)
        for s in range(wslab):
            for t in range(0, tc, 2):
                pair = jnp.concatenate([uslab_ref[s, pl.ds(t, nb, stride=pitch), :],
                                        uslab_ref[s, pl.ds(t + 1, nb, stride=pitch), :]], axis=0)
                utb_ac[t * nb:(t + 2) * nb, s * lanes:(s + 1) * lanes] = pair.astype(BF16)

    pieces = []

    def stage_c(n):
        def run():
            s0 = n * chan_tile_states
            cols = slice(n * tile, (n + 1) * tile)
            yr = _dot(st_ac[:, s0:s0 + chan_tile_states].astype(BF16), cr_ref[n])
            yi = _dot(st_ac[:, nst + s0:nst + s0 + chan_tile_states].astype(BF16), ci_ref[n])
            y = yr - yi + utb_ac[:, cols].astype(F32) * dsk_ref[:, cols]
            g_ref[:, cols] = _gelu(y)
        return run

    def stage_c_out():
        z = _dot(g_ref[...].astype(BF16), wglu_ref[...]) + bglu_ref[...]
        g_ref[...] = g_ref[...] * _sigmoid(z)
        for s in range(wslab):
            for t in range(tc):
                oslab_ref[s, pl.ds(t, nb, stride=pitch), :] = g_ref[t * nb:(t + 1) * nb, s * lanes:(s + 1) * lanes]
        for s in range(wslab):
            for b in range(nb):
                out_ref[b, :, s * lanes:(s + 1) * lanes] = oslab_ref[s, b * pitch:b * pitch + tc, :].astype(BF16)

    def stage_a(j, w_ref, off):
        def run():
            kt = (j * tile // chan_tile_states) * tile
            st_ac[:, off + j * tile:off + (j + 1) * tile] = _dot(utb_ac[:, kt:kt + tile], w_ref[j])
        return run

    pieces += [stage_c(n) for n in range(W // tile)] + [stage_c_out, stage_a_relayout]
    for j in range(nst // tile):
        pieces += [stage_a(j, wbr_ref, 0), stage_a(j, wbi_ref, nst)]

    cw = S5_LANE_CHUNK
    nchunk = nst // cw
    every = (nchunk * tc) // len(pieces)
    slot = 0
    for c in range(nchunk):
        lo = c * cw
        ar = jnp.broadcast_to(ar_ref[:, lo:lo + cw], (nb, cw))
        ai = jnp.broadcast_to(ai_ref[:, lo:lo + cw], (nb, cw))
        hr = h_ref[:, lo:lo + cw]
        hi = h_ref[:, nst + lo:nst + lo + cw]
        for t in range(tc):
            r = slice(t * nb, (t + 1) * nb)
            br = st_b[r, lo:lo + cw]
            bi = st_b[r, nst + lo:nst + lo + cw]
            hr, hi = ar * hr - ai * hi + br, ar * hi + ai * hr + bi
            st_b[r, lo:lo + cw] = hr
            st_b[r, nst + lo:nst + lo + cw] = hi
            slot += 1
            if slot % every == 0 and pieces:
                pieces.pop(0)()
        h_ref[:, lo:lo + cw] = hr
        h_ref[:, nst + lo:nst + lo + cw] = hi
    for piece in pieces:
        piece()


def _s5(u, l, wbr, wbi, ar, ai, cr, ci, dskip, wglu, bglu):
    B, S, W = u.shape
    tc = S5_STEPS
    n = S // tc
    return pl.pallas_call(
        _s5_kernel,
        grid=(n + 2,),
        in_specs=[pl.BlockSpec((B, tc, W), lambda g: (0, jnp.minimum(g, n - 1), 0))]
        + [_const_spec(a.shape) for a in (wbr, wbi, ar, ai, cr, ci, dskip)]
        + [_layer_spec(wglu.shape, l), _const_spec(bglu.shape)],
        out_specs=pl.BlockSpec((B, tc, W), lambda g: (0, jnp.maximum(g - 2, 0), 0)),
        out_shape=jax.ShapeDtypeStruct((B, S, W), BF16),
        scratch_shapes=[
            pltpu.VMEM((B, 2 * SSM_STATES), F32),
            pltpu.VMEM((B * tc, 2 * SSM_STATES), F32),
            pltpu.VMEM((B * tc, 2 * SSM_STATES), F32),
            pltpu.VMEM((B * tc, W), BF16),
            pltpu.VMEM((B * tc, W), BF16),
            pltpu.VMEM((W // V7X_LANES, B * S5_PITCH, V7X_LANES), F32),
            pltpu.VMEM((W // V7X_LANES, B * S5_PITCH, V7X_LANES), F32),
            pltpu.VMEM((B * tc, W), F32),
        ],
        compiler_params=_params("arbitrary"),
        name="s5",
    )(u, wbr, wbi, ar, ai, cr, ci, dskip, wglu, bglu)


def _s5_params(a_re, a_im, log_dt, b_re, b_im, c_re, c_im):
    G, P, C = SSM_GROUPS, SSM_STATE, SSM_GROUP
    lam_re = jnp.minimum(a_re, -1e-4)
    lam_im = a_im
    dt = jnp.exp(log_dt)[:, None]
    mag = jnp.exp(lam_re * dt)
    ab_re, ab_im = mag * jnp.cos(lam_im * dt), mag * jnp.sin(lam_im * dt)
    den = lam_re * lam_re + lam_im * lam_im
    f_re = ((ab_re - 1.0) * lam_re + ab_im * lam_im) / den
    f_im = (ab_im * lam_re - (ab_re - 1.0) * lam_im) / den
    bb_re = f_re[..., None] * b_re - f_im[..., None] * b_im
    bb_im = f_re[..., None] * b_im + f_im[..., None] * b_re
    tile = V7X_MXU_DIM
    gs, gc = tile // P, tile // C
    nts = G // gs
    sel = np.zeros((nts, gs, gc), np.float32)
    for j in range(nts):
        for n in range(gs):
            sel[j, n, (j * gs + n) % gc] = 1.0
    eye = np.eye(gc, dtype=np.float32)

    def in_mat(bb):
        t = jnp.einsum('jnpc,jnk->jkcnp', bb.reshape(nts, gs, P, C), sel)
        return t.reshape(nts, tile, tile).astype(BF16)

    def out_mat(cc):
        t = jnp.einsum('ngcp,gh->ngphc', cc.reshape(G // gc, gc, C, P), eye)
        return t.reshape(G // gc, gc * P, tile).astype(BF16)

    return (in_mat(bb_re), in_mat(bb_im), ab_re.reshape(1, G * P), ab_im.reshape(1, G * P),
            out_mat(c_re), out_mat(c_im))


def _merge_kernel(x_ref, a_ref, o1_ref, o2_ref, o3_ref, s1_ref, s2_ref, s3_ref, c_ref, d_ref,
                  gpre_ref, wg_ref, gb_ref, wup_ref, wout_ref, gpost_ref, out_ref, b_ref, on_ref, sn_ref):
    _attn_mix(o1_ref, o2_ref, o3_ref, s1_ref, s2_ref, s3_ref, b_ref, on_ref, sn_ref)
    x = x_ref[...]
    h = _rms(x, gpre_ref[...]).astype(BF16)
    merged = None
    for i, br in enumerate((a_ref, b_ref, c_ref, d_ref)):
        gate = _sigmoid(_dot(h, wg_ref[:, i * D_MODEL:(i + 1) * D_MODEL]) + gb_ref[i:i + 1, :])
        term = gate * _dot(br[...], wup_ref[i])
        merged = term if merged is None else merged + term
    y = _dot(merged.astype(BF16), wout_ref[...])
    out_ref[...] = x + _rms(y, gpost_ref[...])


def _merge(x, l, a, outs, stats, c, d, gpre, wg, gb, wup, wout, gpost):
    B, S, D = x.shape
    tm = ATT_SPAN
    W = W_BRANCH

    def tok(a_):
        if a_.ndim == 3:
            return pl.BlockSpec((None, tm, a_.shape[-1]), lambda b_, i: (b_, i, 0))
        return pl.BlockSpec((None, None) + a_.shape[2:], lambda b_, i: (b_, i, 0, 0, 0))

    return pl.pallas_call(
        _merge_kernel,
        grid=(B, S // tm),
        in_specs=[tok(a_) for a_ in (x, a, *outs, *stats, c, d)] + [
            _const_spec((1, D)), _layer_spec(wg.shape, l), _const_spec(gb.shape), _layer_spec(wup.shape, l),
            _layer_spec(wout.shape, l), _const_spec((1, D)),
        ],
        out_specs=tok(x),
        out_shape=jax.ShapeDtypeStruct((B, S, D), F32),
        scratch_shapes=[
            pltpu.VMEM((tm, W), BF16),
            pltpu.VMEM((2 * W // V7X_LANES, tm, V7X_LANES), F32),
            pltpu.VMEM((2, tm, V7X_LANES), F32),
        ],
        compiler_params=_params("parallel", "parallel"),
        name="merge",
    )(x, a, *outs, *stats, c, d, gpre, wg, gb, wup, wout, gpost)


def _mem_kv_kernel(mem_ref, g_ref, w_ref, k_ref, v_ref):
    mn = _rms(mem_ref[...], g_ref[...]).astype(BF16)
    hw = X_HEADS * X_HEAD_DIM
    k_ref[...] = _dot(mn, w_ref[:, :hw]).astype(BF16)
    v_ref[...] = _dot(mn, w_ref[:, hw:]).astype(BF16)


def _mem_kv(mem, l, g, w):
    B, M, D = mem.shape
    hw = X_HEADS * X_HEAD_DIM
    ob = pl.BlockSpec((None, M, hw), lambda b: (b, 0, 0))
    return pl.pallas_call(
        _mem_kv_kernel,
        grid=(B,),
        in_specs=[pl.BlockSpec((None, M, D), lambda b: (b, 0, 0)), _const_spec((1, D)), _layer_spec(w.shape, l)],
        out_specs=[ob, ob],
        out_shape=[jax.ShapeDtypeStruct((B, M, hw), BF16)] * 2,
        compiler_params=_params("parallel"),
        name="mem_kv",
    )(mem, g, w)


def _xattn_ffn_kernel(x_ref, k_ref, v_ref, gxpre_ref, wq_ref, wo_ref, gxpost_ref, gfpre_ref, w1_ref, w2_ref,
                      gfpost_ref, out_ref, o_ref, acc_ref):
    x = x_ref[...]
    h = _rms(x, gxpre_ref[...]).astype(BF16)
    q = (_dot(h, wq_ref[...]) * (X_HEAD_DIM ** -0.5)).astype(BF16)
    heads = [slice(hd * X_HEAD_DIM, (hd + 1) * X_HEAD_DIM) for hd in range(X_HEADS)]
    scores = [lax.dot_general(q[:, cols], k_ref[:, cols], (((1,), (1,)), ((), ())), preferred_element_type=F32)
              for cols in heads]
    for cols, s in zip(heads, scores):
        m = jnp.max(s, axis=-1, keepdims=True)
        p = jnp.exp(s - m)
        l = jnp.sum(p, axis=-1, keepdims=True)
        o_ref[:, cols] = (_dot(p.astype(BF16), v_ref[:, cols]) * (1.0 / l)).astype(BF16)
    x = x + _rms(_dot(o_ref[...], wo_ref[...]), gxpost_ref[...])
    h = _rms(x, gfpre_ref[...]).astype(BF16)
    for c in range(D_FF // FF_CHUNK):
        cols = slice(c * FF_CHUNK, (c + 1) * FF_CHUNK)
        a = jnp.maximum(_dot(h, w1_ref[:, cols]), 0.0)
        a = (a * a).astype(BF16)
        part = _dot(a, w2_ref[cols, :])
        if c == 0:
            acc_ref[...] = part
        else:
            acc_ref[...] += part
    out_ref[...] = x + _rms(acc_ref[...], gfpost_ref[...])


def _xattn_ffn(x, l, k, v, gxpre, wq, wo, gxpost, gfpre, w1, w2, gfpost):
    B, S, D = x.shape
    tm = TOKEN_BLOCK
    hw = X_HEADS * X_HEAD_DIM
    xb = pl.BlockSpec((None, tm, D), lambda b, i: (b, i, 0))
    kb = pl.BlockSpec((None, N_MEM, hw), lambda b, i: (b, 0, 0))
    consts = (gxpre, wq, wo, gxpost, gfpre, w1, w2, gfpost)
    return pl.pallas_call(
        _xattn_ffn_kernel,
        grid=(B, S // tm),
        in_specs=[xb, kb, kb] + [_layer_spec(a.shape, l) if a.ndim == 3 else _const_spec(a.shape) for a in consts],
        out_specs=xb,
        out_shape=jax.ShapeDtypeStruct((B, S, D), F32),
        scratch_shapes=[pltpu.VMEM((tm, hw), BF16), pltpu.VMEM((tm, D), F32)],
        compiler_params=_params("parallel", "parallel"),
        name="xattn_ffn",
    )(x, k, v, *consts)


def _t5_bucket(n):
    exact = REL_BUCKETS // 2
    nf = np.maximum(n, 1).astype(np.float32)
    large = exact + (np.log(nf / exact) / np.log(REL_MAX_DIST / exact) * (REL_BUCKETS - exact)).astype(np.int32)
    large = np.minimum(large, REL_BUCKETS - 1)
    return np.where(n < exact, n, large).astype(np.int32)


def _band_bias(rel_bias, g, band, dil):
    d = np.arange(band + 1)
    table = rel_bias[jnp.asarray(_t5_bucket(d * dil))][:, g * ATT_HEADS:(g + 1) * ATT_HEADS].astype(F32) * LOG2E
    pad = jnp.full((ATT_HEADS, band - 1), NEG_INF, F32)
    f = jnp.concatenate([pad, table[::-1].T, pad], axis=1)
    full = jnp.stack([f[:, band - 1 - i:3 * band - 1 - i] for i in range(band)], axis=1)
    no_prev = jnp.concatenate([jnp.full((ATT_HEADS, band, band), NEG_INF, F32), full[:, :, band:]], axis=2)
    return jnp.stack([full, no_prev], axis=0)


def kernel(x, mem, rel_bias, g_mix_pre, g_mix_post, w_in, gate_b, pool_w, pool_scale, a_re, a_im, log_dt,
           b_re, b_im, c_re, c_im, d_skip, w_glu, b_glu, sgu_ln_g, sgu_ln_b, w_s, b_s, w_up, w_out,
           g_x_pre, g_x_post, g_mem, w_cq, w_ckv, w_co, g_ff_pre, g_ff_post, w_ff1, w_ff2):
    B, S, D = x.shape
    depth = w_in.shape[0]
    assert D == D_MODEL and S % (TOKEN_BLOCK) == 0 and B == V7X_SUBLANES
    for win, dil in DIL_GROUPS:
        assert win // dil == ATT_BAND and (S // dil) % ATT_BAND == 0

    biases = [_band_bias(rel_bias, g, win // dil, dil) for g, (win, dil) in enumerate(DIL_GROUPS)]
    row = lambda a: a.reshape(1, -1).astype(F32)
    w_mix, w_gate = w_in[:, :, :OFF_GATE].astype(BF16), w_in[:, :, OFF_GATE:].astype(BF16)
    w_up, w_out, w_glu, w_cq, w_ckv, w_co, w_ff1, w_ff2 = (
        a.astype(BF16) for a in (w_up, w_out, w_glu, w_cq, w_ckv, w_co, w_ff1, w_ff2))

    for l in range(depth):
        a_out, qkv1, qkv2, qkv3, ssm_in, d_out = _in_proj(
            x, l, row(g_mix_pre[l]), w_mix, pool_w[l].astype(BF16), row(pool_scale[l]),
            row(sgu_ln_g[l]), row(sgu_ln_b[l]), w_s[l], b_s[l].T)
        s5p = _s5_params(a_re[l], a_im[l], log_dt[l], b_re[l], b_im[l], c_re[l], c_im[l])
        c_out = _s5(ssm_in, l, *s5p, row(d_skip[l]), w_glu, row(b_glu[l]))
        outs, stats = zip(*[_attn_group(qkv, bias, dil)
                            for qkv, bias, (win, dil) in zip((qkv1, qkv2, qkv3), biases, DIL_GROUPS)])
        x = _merge(x, l, a_out, outs, stats, c_out, d_out, row(g_mix_pre[l]), w_gate, gate_b[l], w_up, w_out,
                   row(g_mix_post[l]))
        k_mem, v_mem = _mem_kv(mem, l, row(g_mem[l]), w_ckv)
        x = _xattn_ffn(x, l, k_mem, v_mem, row(g_x_pre[l]), w_cq, w_co, row(g_x_post[l]),
                       row(g_ff_pre[l]), w_ff1, w_ff2, row(g_ff_post[l]))
    return x
```

```python
import functools
import math

import jax
import jax.numpy as jnp
import numpy as np
from jax import lax
from jax.experimental import pallas as pl
from jax.experimental.pallas import tpu as pltpu

F32 = jnp.float32
BF16 = jnp.bfloat16

D_MODEL = 1024
N_MEM = 256
N_BRANCH = 4
W_BRANCH = D_MODEL // 2
POOL_WINDOWS = (2, 4, 8, 16)
POOL_GROUP = W_BRANCH // len(POOL_WINDOWS)
DIL_GROUPS = ((128, 1), (512, 4), (2048, 16))
ATT_HEADS = 8
ATT_HEAD_DIM = W_BRANCH // ATT_HEADS
SSM_GROUP = 16
SSM_GROUPS = W_BRANCH // SSM_GROUP
SSM_STATE = 64
SSM_STATES = SSM_GROUPS * SSM_STATE
SGU_CHUNK = 128
SGU_GROUPS = 4
X_HEADS = 4
X_HEAD_DIM = 128
D_FF = 4 * D_MODEL
REL_BUCKETS = 32
REL_MAX_DIST = 2048
EPS = 1e-6
NEG_INF = -1e30
N_ATT_COLS = 3 * len(DIL_GROUPS) * W_BRANCH
OFF_GATE = W_BRANCH + N_ATT_COLS + W_BRANCH + 2 * W_BRANCH

V7X_LANES = 128
V7X_SUBLANES = 8
V7X_MXU_DIM = 256
V7X_VMEM_LIMIT_BYTES = 56 * 1024 * 1024

TOKEN_BLOCK = 1024
POOL_HALO = 16
ATT_BAND = 128
ATT_SPAN = 512
ATT_UNITS = 16
IN_PROJ_SLAB_SETS = 3
LOG2E = 1.4426950408889634
ATT_Q_SCALE = ATT_HEAD_DIM ** -0.5 * LOG2E
S5_STEPS = 64
S5_PITCH = S5_STEPS + 8
S5_LANE_CHUNK = 512
FF_CHUNK = 512


def _params(*sem):
    return pltpu.CompilerParams(dimension_semantics=sem, vmem_limit_bytes=V7X_VMEM_LIMIT_BYTES)


def _const_spec(shape):
    nd = len(shape)
    return pl.BlockSpec(shape, lambda *_: (0,) * nd, pipeline_mode=pl.Buffered(1))


def _layer_spec(shape, l):
    nd = len(shape) - 1
    return pl.BlockSpec((None,) + tuple(shape[1:]), lambda *_: (l,) + (0,) * nd, pipeline_mode=pl.Buffered(1))


def _rms(xf, g):
    return xf * lax.rsqrt(jnp.mean(xf * xf, axis=-1, keepdims=True) + EPS) * g


def _gelu(x):
    c = math.sqrt(2.0 / math.pi)
    return 0.5 * x * (1.0 + jnp.tanh(c * (x + 0.044715 * (x * x * x))))


def _sigmoid(x):
    return 1.0 / (1.0 + jnp.exp(-x))


def _dot(a, b):
    return jnp.dot(a, b, preferred_element_type=F32)


def _in_proj_kernel(x_ref, g_ref, w_ref, poolw_ref, pscale_ref, lng_ref, lnb_ref, ws_ref, bst_ref,
                    a_ref, qkv1_ref, qkv2_ref, qkv3_ref, ssm_ref, d_ref, slab_ref, ext_ref):
    tm = x_ref.shape[0]
    step = W_BRANCH
    ng = len(DIL_GROUPS)
    nslab = step // V7X_LANES
    i = pl.program_id(1)

    @pl.when(i == 0)
    def _():
        ext_ref[0:POOL_HALO, :] = jnp.zeros((POOL_HALO, step), F32)

    @pl.when(i > 0)
    def _():
        ext_ref[0:POOL_HALO, :] = ext_ref[tm:tm + POOL_HALO, :]

    h = _rms(x_ref[...], g_ref[...]).astype(BF16)
    proj = lambda lo: _dot(h, w_ref[:, lo:lo + step])

    lo_sgu = W_BRANCH + N_ATT_COLS + W_BRANCH
    u = _gelu(proj(lo_sgu))
    v = _gelu(proj(lo_sgu + step))
    vc = v - jnp.mean(v, axis=-1, keepdims=True)
    var = jnp.mean(vc * vc, axis=-1, keepdims=True)
    vn = (vc * lax.rsqrt(var + EPS) * lng_ref[...] + lnb_ref[...]).astype(BF16)

    ext_ref[POOL_HALO:, :] = proj(0)

    def qkv(which):
        dst = slice(which * step, (which + 1) * step)
        for g, out_ref in enumerate((qkv1_ref, qkv2_ref, qkv3_ref)):
            y = proj(W_BRANCH + (which * ng + g) * step)
            if which == 0:
                y = y * ATT_Q_SCALE
            dil = DIL_GROUPS[g][1]
            if dil == 1:
                out_ref[:, dst] = y.astype(BF16)
                continue
            base = ((which * (ng - 1) + g - 1) % IN_PROJ_SLAB_SETS) * nslab
            for c in range(nslab):
                slab_ref[base + c] = y[:, c * V7X_LANES:(c + 1) * V7X_LANES]
            for c in range(nslab):
                cols = slice(which * step + c * V7X_LANES, which * step + (c + 1) * V7X_LANES)
                for r in range(dil):
                    out_ref[r, :, cols] = slab_ref[base + c, pl.ds(r, tm // dil, stride=dil), :].astype(BF16)

    qkv(0)

    tpos = i * tm + lax.broadcasted_iota(jnp.int32, (tm, 1), 0)
    for gi, win in enumerate(POOL_WINDOWS):
        lanes = slice(gi * POOL_GROUP, (gi + 1) * POOL_GROUP)
        tok = ext_ref[POOL_HALO:POOL_HALO + tm, lanes]
        s = tok
        for j in range(1, win):
            s = s + ext_ref[POOL_HALO - j:POOL_HALO - j + tm, lanes]
        cnt = jnp.minimum(tpos + 1, win).astype(F32)
        p = s / cnt - tok
        a_ref[:, lanes] = (_dot(p.astype(BF16), poolw_ref[gi]) * pscale_ref[:, lanes]).astype(BF16)

    qkv(1)

    T = SGU_CHUNK
    gd = W_BRANCH // SGU_GROUPS
    causal = lax.broadcasted_iota(jnp.int32, (T, T), 0) >= lax.broadcasted_iota(jnp.int32, (T, T), 1)
    for g in range(SGU_GROUPS):
        wg = jnp.where(causal, ws_ref[g], 0.0).astype(BF16)
        bias = bst_ref[:, g:g + 1]
        cols = slice(g * gd, (g + 1) * gd)
        for c in range(tm // T):
            rows = slice(c * T, (c + 1) * T)
            sv = _dot(wg, vn[rows, cols]) + bias
            d_ref[rows, cols] = (u[rows, cols] * sv).astype(BF16)

    qkv(2)
    ssm_ref[...] = proj(W_BRANCH + N_ATT_COLS).astype(BF16)


def _residue_major_shape(B, S, dil, width):
    return (B, S // ATT_SPAN, dil, ATT_SPAN // dil, width)


def _in_proj(x, l, g, w, pool_w, pool_scale, ln_g, ln_b, w_s, b_s_t):
    B, S, D = x.shape
    tm = ATT_SPAN
    qw = 3 * W_BRANCH
    d2, d3 = DIL_GROUPS[1][1], DIL_GROUPS[2][1]
    tok = lambda width: pl.BlockSpec((None, tm, width), lambda b, i: (b, i, 0))
    return pl.pallas_call(
        _in_proj_kernel,
        grid=(B, S // tm),
        in_specs=[
            tok(D),
            _const_spec((1, D)),
            _layer_spec(w.shape, l),
            _const_spec(pool_w.shape), _const_spec(pool_scale.shape), _const_spec(ln_g.shape),
            _const_spec(ln_b.shape), _const_spec(w_s.shape), _const_spec(b_s_t.shape),
        ],
        out_specs=[
            tok(W_BRANCH),
            tok(qw),
            pl.BlockSpec((None, None, d2, tm // d2, qw), lambda b, i: (b, i, 0, 0, 0)),
            pl.BlockSpec((None, None, d3, tm // d3, qw), lambda b, i: (b, i, 0, 0, 0)),
            tok(W_BRANCH),
            tok(W_BRANCH),
        ],
        out_shape=[
            jax.ShapeDtypeStruct((B, S, W_BRANCH), BF16),
            jax.ShapeDtypeStruct((B, S, qw), BF16),
            jax.ShapeDtypeStruct(_residue_major_shape(B, S, d2, qw), BF16),
            jax.ShapeDtypeStruct(_residue_major_shape(B, S, d3, qw), BF16),
            jax.ShapeDtypeStruct((B, S, W_BRANCH), BF16),
            jax.ShapeDtypeStruct((B, S, W_BRANCH), BF16),
        ],
        scratch_shapes=[
            pltpu.VMEM((IN_PROJ_SLAB_SETS * W_BRANCH // V7X_LANES, tm, V7X_LANES), F32),
            pltpu.VMEM((POOL_HALO + tm, W_BRANCH), F32),
        ],
        compiler_params=_params("parallel", "arbitrary"),
        name="in_proj",
    )(x, g, w, pool_w, pool_scale, ln_g, ln_b, w_s, b_s_t)


def _attn_tasks(layout, first_step, q_ref, kp_ref, kc_ref, vp_ref, vc_ref, bias_ref, ind_ref, o_ref, st_ref):
    band, W = ATT_BAND, W_BRANCH
    lane = lax.broadcasted_iota(jnp.int32, (1, V7X_LANES), 1)
    low = lane < ATT_HEAD_DIM
    cache = {}

    def unit(u):
        if u in cache:
            return cache[u]
        first_table = jnp.where(first_step, 1, 0)
        if layout == "natural":
            rows = slice(u * band, (u + 1) * band)
            if u == 0:
                k = jnp.concatenate([kp_ref[...], kc_ref[rows, :]], axis=0)
                v = jnp.concatenate([vp_ref[...], vc_ref[rows, :]], axis=0)
            else:
                k, v = kc_ref[(u - 1) * band:(u + 1) * band, :], vc_ref[(u - 1) * band:(u + 1) * band, :]
            cache[u] = (q_ref[rows, :], k, v, first_table if u == 0 else 0,
                        lambda cols, val: o_ref.__setitem__((rows, cols), val),
                        lambda val: st_ref.__setitem__((rows, slice(None)), val))
        else:
            per = q_ref.shape[2]
            spb = band // per
            bb, r = divmod(u, q_ref.shape[1])
            cur = slice(bb * spb, (bb + 1) * spb)
            get = lambda ref, sp: ref[sp, r].reshape(band, W)
            if bb == 0:
                prev = lambda p, c: get(p, slice(None))
            else:
                prev = lambda p, c: get(c, slice((bb - 1) * spb, bb * spb))
            cache[u] = (get(q_ref, cur),
                        jnp.concatenate([prev(kp_ref, kc_ref), get(kc_ref, cur)], axis=0),
                        jnp.concatenate([prev(vp_ref, vc_ref), get(vc_ref, cur)], axis=0),
                        first_table if bb == 0 else 0,
                        lambda cols, val: o_ref.__setitem__((cur, r, slice(None), cols),
                                                            val.reshape(spb, per, V7X_LANES)),
                        lambda val: st_ref.__setitem__((cur, r), val.reshape(spb, per, V7X_LANES)))
        return cache[u]

    def score(u, j):
        q, k, _, table, _, _ = unit(u)
        cols = slice(j * V7X_LANES, (j + 1) * V7X_LANES)
        qp = q[:, cols]
        zq = jnp.zeros_like(qp)
        q2 = jnp.concatenate([jnp.where(low, qp, zq), jnp.where(low, zq, qp)], axis=0)
        s = lax.dot_general(q2, k[:, cols], (((1,), (1,)), ((), ())), preferred_element_type=F32)
        return s + jnp.concatenate([bias_ref[table, 2 * j], bias_ref[table, 2 * j + 1]], axis=0)

    n_units = q_ref.shape[0] // band if layout == "natural" else q_ref.shape[0] * q_ref.shape[1] * q_ref.shape[2] // band
    todo = [(u, j) for u in range(n_units) for j in range(ATT_HEADS // 2)]
    live = {}

    def task(idx):
        def run():
            u, j = todo[idx]
            _, _, v, _, put_o, put_st = unit(u)
            s = live.pop("score") if "score" in live else score(u, j)
            if idx + 1 < len(todo):
                live["score"] = score(*todo[idx + 1])
            stats = jnp.zeros((band, V7X_LANES), F32) if j == 0 else live.pop("stats")
            cols = slice(j * V7X_LANES, (j + 1) * V7X_LANES)
            m = jnp.max(s, axis=-1, keepdims=True)
            pb = jnp.exp2(s - m).astype(BF16)
            p2 = jnp.concatenate([pb[:band], pb[band:]], axis=1)
            vp = v[:, cols]
            zv = jnp.zeros_like(vp)
            v2 = jnp.concatenate([jnp.where(low, vp, zv), jnp.where(low, zv, vp)], axis=0)
            pv = _dot(p2, jnp.concatenate([v2, ind_ref[j]], axis=1))
            put_o(cols, pv[:, :V7X_LANES].astype(BF16))
            stats = stats + pv[:, V7X_LANES:]
            for e in range(2):
                stats = jnp.where(lane == 2 * j + e, m[e * band:(e + 1) * band], stats)
            if j == ATT_HEADS // 2 - 1:
                put_st(stats)
            else:
                live["stats"] = stats
        return run

    return [task(idx) for idx in range(len(todo))]


def _attn_specs(qkv, dil, n_items):
    B = qkv.shape[0]
    W = W_BRANCH
    U = ATT_UNITS
    item = lambda g: jnp.minimum(g, n_items - 1)
    if dil == 1:
        nspan = qkv.shape[1] // (U * ATT_BAND)
        layout = "natural"
        blk = lambda width: (None, U * ATT_BAND, width)
        prev_blk = (None, ATT_BAND, W)
        pos = lambda g: (item(g) // nspan, item(g) % nspan)
        first = lambda g: pos(g)[1] == 0
        idx = lambda which: (lambda g: (*pos(g), which))
        idx_prev = lambda which: (lambda g: (pos(g)[0], jnp.maximum(U * pos(g)[1] - 1, 0), which))
        out_shape = lambda width: (B, qkv.shape[1], width)
    else:
        nspan, _, per, _ = qkv.shape[1:]
        spb = ATT_BAND // per
        res = min(dil, U)
        bands = U // res
        rgroups, nblk = dil // res, nspan // (spb * bands)
        layout = "residue-major"
        blk = lambda width: (None, spb * bands, res, per, width)
        prev_blk = (None, spb, res, per, W)
        pos = lambda g: (item(g) // (nblk * rgroups), (item(g) // rgroups) % nblk, item(g) % rgroups)
        first = lambda g: pos(g)[1] == 0
        idx = lambda which: (lambda g: (*pos(g), 0, which))
        idx_prev = lambda which: (lambda g: (pos(g)[0], jnp.maximum(pos(g)[1] * bands - 1, 0), pos(g)[2], 0, which))
        out_shape = lambda width: (B, nspan, dil, per, width)
        assert B * nblk * rgroups == n_items
    in_specs = [pl.BlockSpec(blk(W), idx(0)), pl.BlockSpec(prev_blk, idx_prev(1)), pl.BlockSpec(blk(W), idx(1)),
                pl.BlockSpec(prev_blk, idx_prev(2)), pl.BlockSpec(blk(W), idx(2))]
    return layout, first, in_specs, (lambda width: pl.BlockSpec(blk(width), idx(0))), out_shape


def _attn_kernel(layout, first, *refs):
    for task in _attn_tasks(layout, first(pl.program_id(0)), *refs):
        task()


def _attn_group(qkv, bias, dil):
    n_items = qkv.shape[0] * (qkv.shape[1] * qkv.shape[2] * qkv.shape[3] if qkv.ndim == 5 else qkv.shape[1]) // (ATT_UNITS * ATT_BAND)
    key_head = np.arange(4 * ATT_BAND)[:, None] // (2 * ATT_BAND)
    ind = jnp.asarray(np.stack([np.arange(V7X_LANES)[None, :] == ATT_HEADS + 2 * j + key_head
                                for j in range(ATT_HEADS // 2)]), BF16)
    layout, first, in_specs, out_spec, out_shape = _attn_specs(qkv, dil, n_items)
    return pl.pallas_call(
        functools.partial(_attn_kernel, layout, first),
        grid=(n_items,),
        in_specs=in_specs + [_const_spec(bias.shape), _const_spec(ind.shape)],
        out_specs=[out_spec(W_BRANCH), out_spec(V7X_LANES)],
        out_shape=[jax.ShapeDtypeStruct(out_shape(W_BRANCH), BF16),
                   jax.ShapeDtypeStruct(out_shape(V7X_LANES), F32)],
        compiler_params=_params("parallel"),
        name=f"attn_d{dil}",
    )(qkv, qkv, qkv, qkv, qkv, bias, ind)


def _attn_mix(o1_ref, o2_ref, o3_ref, s1_ref, s2_ref, s3_ref, out_ref, on_ref, sn_ref):
    lane = lax.broadcasted_iota(jnp.int32, (1, V7X_LANES), 1)
    low = lane < ATT_HEAD_DIM
    nslab = W_BRANCH // V7X_LANES
    for gi, (o_ref, s_ref) in enumerate(((o2_ref, s2_ref), (o3_ref, s3_ref))):
        dil, per = o_ref.shape[0], o_ref.shape[1]
        for r in range(dil):
            sn_ref[gi, pl.ds(r, per, stride=dil), :] = s_ref[r]
            for c in range(nslab):
                on_ref[gi * nslab + c, pl.ds(r, per, stride=dil), :] = (
                    o_ref[r, :, c * V7X_LANES:(c + 1) * V7X_LANES].astype(F32))
    stats = (s1_ref[...], sn_ref[0], sn_ref[1])
    for j in range(ATT_HEADS // 2):
        cols = slice(j * V7X_LANES, (j + 1) * V7X_LANES)
        wts = []
        for e in range(2):
            h = 2 * j + e
            ms = [x[:, h:h + 1] for x in stats]
            ls = [x[:, ATT_HEADS + h:ATT_HEADS + h + 1] for x in stats]
            top = jnp.maximum(jnp.maximum(ms[0], ms[1]), ms[2])
            ex = [jnp.exp2(x - top) for x in ms]
            inv = 1.0 / (ex[0] * ls[0] + ex[1] * ls[1] + ex[2] * ls[2])
            wts.append([x * inv for x in ex])
        vals = (o1_ref[:, cols].astype(F32), on_ref[j], on_ref[nslab + j])
        acc = None
        for gi in range(3):
            term = jnp.where(low, wts[0][gi], wts[1][gi]) * vals[gi]
            acc = term if acc is None else acc + term
        out_ref[:, cols] = acc.astype(BF16)


def _s5_kernel(*refs):
    n_io = 11
    h_ref, st_even, st_odd, utb_even, utb_odd = refs[n_io:n_io + 5]
    io, shared = refs[:n_io], refs[n_io + 5:]
    g = pl.program_id(0)

    @pl.when(g == 0)
    def _():
        h_ref[...] = jnp.zeros_like(h_ref)
        for ref in (st_even, st_odd, utb_even, utb_odd):
            ref[...] = jnp.zeros_like(ref)

    @pl.when(lax.rem(g, 2) == 0)
    def _():
        _s5_step(h_ref, st_even, st_odd, utb_even, *io, *shared)

    @pl.when(lax.rem(g, 2) == 1)
    def _():
        _s5_step(h_ref, st_odd, st_even, utb_odd, *io, *shared)


def _s5_step(h_ref, st_ac, st_b, utb_ac, u_ref, wbr_ref, wbi_ref, ar_ref, ai_ref, cr_ref, ci_ref, dsk_ref,
             wglu_ref, bglu_ref, out_ref, uslab_ref, oslab_ref, g_ref):
    nst = SSM_STATES
    nb, tc, W = u_ref.shape
    pitch = S5_PITCH
    lanes = V7X_LANES
    wslab = W // lanes
    tile = V7X_MXU_DIM
    chan_tile_states = tile * SSM_STATE // SSM_GROUP

    def stage_a_relayout():
        for s in range(wslab):
            for b in range(nb):
                uslab_ref[s, b * pitch:b * pitch + tc, :] = u_ref[b, :, s * lanes:(s + 1) * lanes].astype(F32)
        for s in range(wslab):
            for t in range(0, tc, 2):
                pair = jnp.concatenate([uslab_ref[s, pl.ds(t, nb, stride=pitch), :],
                                        uslab_ref[s, pl.ds(t + 1, nb, stride=pitch), :]], axis=0)
                utb_ac[t * nb:(t + 2) * nb, s * lanes:(s + 1) * lanes] = pair.astype(BF16)

    pieces = []

    def stage_c(n):
        def run():
            s0 = n * chan_tile_states
            cols = slice(n * tile, (n + 1) * tile)
            yr = _dot(st_ac[:, s0:s0 + chan_tile_states].astype(BF16), cr_ref[n])
            yi = _dot(st_ac[:, nst + s0:nst + s0 + chan_tile_states].astype(BF16), ci_ref[n])
            y = yr - yi + utb_ac[:, cols].astype(F32) * dsk_ref[:, cols]
            g_ref[:, cols] = _gelu(y)
        return run

    def stage_c_out():
        z = _dot(g_ref[...].astype(BF16), wglu_ref[...]) + bglu_ref[...]
        g_ref[...] = g_ref[...] * _sigmoid(z)
        for s in range(wslab):
            for t in range(tc):
                oslab_ref[s, pl.ds(t, nb, stride=pitch), :] = g_ref[t * nb:(t + 1) * nb, s * lanes:(s + 1) * lanes]
        for s in range(wslab):
            for b in range(nb):
                out_ref[b, :, s * lanes:(s + 1) * lanes] = oslab_ref[s, b * pitch:b * pitch + tc, :].astype(BF16)

    def stage_a(j, w_ref, off):
        def run():
            kt = (j * tile // chan_tile_states) * tile
            st_ac[:, off + j * tile:off + (j + 1) * tile] = _dot(utb_ac[:, kt:kt + tile], w_ref[j])
        return run

    pieces += [stage_c(n) for n in range(W // tile)] + [stage_c_out, stage_a_relayout]
    for j in range(nst // tile):
        pieces += [stage_a(j, wbr_ref, 0), stage_a(j, wbi_ref, nst)]

    cw = S5_LANE_CHUNK
    nchunk = nst // cw
    every = (nchunk * tc) // len(pieces)
    slot = 0
    for c in range(nchunk):
        lo = c * cw
        ar = jnp.broadcast_to(ar_ref[:, lo:lo + cw], (nb, cw))
        ai = jnp.broadcast_to(ai_ref[:, lo:lo + cw], (nb, cw))
        hr = h_ref[:, lo:lo + cw]
        hi = h_ref[:, nst + lo:nst + lo + cw]
        for t in range(tc):
            r = slice(t * nb, (t + 1) * nb)
            br = st_b[r, lo:lo + cw]
            bi = st_b[r, nst + lo:nst + lo + cw]
            hr, hi = ar * hr - ai * hi + br, ar * hi + ai * hr + bi
            st_b[r, lo:lo + cw] = hr
            st_b[r, nst + lo:nst + lo + cw] = hi
            slot += 1
            if slot % every == 0 and pieces:
                pieces.pop(0)()
        h_ref[:, lo:lo + cw] = hr
        h_ref[:, nst + lo:nst + lo + cw] = hi
    for piece in pieces:
        piece()


def _s5(u, l, wbr, wbi, ar, ai, cr, ci, dskip, wglu, bglu):
    B, S, W = u.shape
    tc = S5_STEPS
    n = S // tc
    return pl.pallas_call(
        _s5_kernel,
        grid=(n + 2,),
        in_specs=[pl.BlockSpec((B, tc, W), lambda g: (0, jnp.minimum(g, n - 1), 0))]
        + [_const_spec(a.shape) for a in (wbr, wbi, ar, ai, cr, ci, dskip)]
        + [_layer_spec(wglu.shape, l), _const_spec(bglu.shape)],
        out_specs=pl.BlockSpec((B, tc, W), lambda g: (0, jnp.maximum(g - 2, 0), 0)),
        out_shape=jax.ShapeDtypeStruct((B, S, W), BF16),
        scratch_shapes=[
            pltpu.VMEM((B, 2 * SSM_STATES), F32),
            pltpu.VMEM((B * tc, 2 * SSM_STATES), F32),
            pltpu.VMEM((B * tc, 2 * SSM_STATES), F32),
            pltpu.VMEM((B * tc, W), BF16),
            pltpu.VMEM((B * tc, W), BF16),
            pltpu.VMEM((W // V7X_LANES, B * S5_PITCH, V7X_LANES), F32),
            pltpu.VMEM((W // V7X_LANES, B * S5_PITCH, V7X_LANES), F32),
            pltpu.VMEM((B * tc, W), F32),
        ],
        compiler_params=_params("arbitrary"),
        name="s5",
    )(u, wbr, wbi, ar, ai, cr, ci, dskip, wglu, bglu)


def _s5_params(a_re, a_im, log_dt, b_re, b_im, c_re, c_im):
    G, P, C = SSM_GROUPS, SSM_STATE, SSM_GROUP
    lam_re = jnp.minimum(a_re, -1e-4)
    lam_im = a_im
    dt = jnp.exp(log_dt)[:, None]
    mag = jnp.exp(lam_re * dt)
    ab_re, ab_im = mag * jnp.cos(lam_im * dt), mag * jnp.sin(lam_im * dt)
    den = lam_re * lam_re + lam_im * lam_im
    f_re = ((ab_re - 1.0) * lam_re + ab_im * lam_im) / den
    f_im = (ab_im * lam_re - (ab_re - 1.0) * lam_im) / den
    bb_re = f_re[..., None] * b_re - f_im[..., None] * b_im
    bb_im = f_re[..., None] * b_im + f_im[..., None] * b_re
    tile = V7X_MXU_DIM
    gs, gc = tile // P, tile // C
    nts = G // gs
    sel = np.zeros((nts, gs, gc), np.float32)
    for j in range(nts):
        for n in range(gs):
            sel[j, n, (j * gs + n) % gc] = 1.0
    eye = np.eye(gc, dtype=np.float32)

    def in_mat(bb):
        t = jnp.einsum('jnpc,jnk->jkcnp', bb.reshape(nts, gs, P, C), sel)
        return t.reshape(nts, tile, tile).astype(BF16)

    def out_mat(cc):
        t = jnp.einsum('ngcp,gh->ngphc', cc.reshape(G // gc, gc, C, P), eye)
        return t.reshape(G // gc, gc * P, tile).astype(BF16)

    return (in_mat(bb_re), in_mat(bb_im), ab_re.reshape(1, G * P), ab_im.reshape(1, G * P),
            out_mat(c_re), out_mat(c_im))


def _merge_kernel(x_ref, a_ref, o1_ref, o2_ref, o3_ref, s1_ref, s2_ref, s3_ref, c_ref, d_ref,
                  gpre_ref, wg_ref, gb_ref, wup_ref, wout_ref, gpost_ref, out_ref, b_ref, on_ref, sn_ref):
    _attn_mix(o1_ref, o2_ref, o3_ref, s1_ref, s2_ref, s3_ref, b_ref, on_ref, sn_ref)
    x = x_ref[...]
    h = _rms(x, gpre_ref[...]).astype(BF16)
    merged = None
    for i, br in enumerate((a_ref, b_ref, c_ref, d_ref)):
        gate = _sigmoid(_dot(h, wg_ref[:, i * D_MODEL:(i + 1) * D_MODEL]) + gb_ref[i:i + 1, :])
        term = gate * _dot(br[...], wup_ref[i])
        merged = term if merged is None else merged + term
    y = _dot(merged.astype(BF16), wout_ref[...])
    out_ref[...] = x + _rms(y, gpost_ref[...])


def _merge(x, l, a, outs, stats, c, d, gpre, wg, gb, wup, wout, gpost):
    B, S, D = x.shape
    tm = ATT_SPAN
    W = W_BRANCH

    def tok(a_):
        if a_.ndim == 3:
            return pl.BlockSpec((None, tm, a_.shape[-1]), lambda b_, i: (b_, i, 0))
        return pl.BlockSpec((None, None) + a_.shape[2:], lambda b_, i: (b_, i, 0, 0, 0))

    return pl.pallas_call(
        _merge_kernel,
        grid=(B, S // tm),
        in_specs=[tok(a_) for a_ in (x, a, *outs, *stats, c, d)] + [
            _const_spec((1, D)), _layer_spec(wg.shape, l), _const_spec(gb.shape), _layer_spec(wup.shape, l),
            _layer_spec(wout.shape, l), _const_spec((1, D)),
        ],
        out_specs=tok(x),
        out_shape=jax.ShapeDtypeStruct((B, S, D), F32),
        scratch_shapes=[
            pltpu.VMEM((tm, W), BF16),
            pltpu.VMEM((2 * W // V7X_LANES, tm, V7X_LANES), F32),
            pltpu.VMEM((2, tm, V7X_LANES), F32),
        ],
        compiler_params=_params("parallel", "parallel"),
        name="merge",
    )(x, a, *outs, *stats, c, d, gpre, wg, gb, wup, wout, gpost)


def _mem_kv_kernel(mem_ref, g_ref, w_ref, k_ref, v_ref):
    mn = _rms(mem_ref[...], g_ref[...]).astype(BF16)
    hw = X_HEADS * X_HEAD_DIM
    k_ref[...] = _dot(mn, w_ref[:, :hw]).astype(BF16)
    v_ref[...] = _dot(mn, w_ref[:, hw:]).astype(BF16)


def _mem_kv(mem, l, g, w):
    B, M, D = mem.shape
    hw = X_HEADS * X_HEAD_DIM
    ob = pl.BlockSpec((None, M, hw), lambda b: (b, 0, 0))
    return pl.pallas_call(
        _mem_kv_kernel,
        grid=(B,),
        in_specs=[pl.BlockSpec((None, M, D), lambda b: (b, 0, 0)), _const_spec((1, D)), _layer_spec(w.shape, l)],
        out_specs=[ob, ob],
        out_shape=[jax.ShapeDtypeStruct((B, M, hw), BF16)] * 2,
        compiler_params=_params("parallel"),
        name="mem_kv",
    )(mem, g, w)


def _xattn_ffn_kernel(x_ref, k_ref, v_ref, gxpre_ref, wq_ref, wo_ref, gxpost_ref, gfpre_ref, w1_ref, w2_ref,
                      gfpost_ref, out_ref, o_ref, acc_ref):
    x = x_ref[...]
    h = _rms(x, gxpre_ref[...]).astype(BF16)
    q = (_dot(h, wq_ref[...]) * (X_HEAD_DIM ** -0.5)).astype(BF16)
    heads = [slice(hd * X_HEAD_DIM, (hd + 1) * X_HEAD_DIM) for hd in range(X_HEADS)]
    scores = [lax.dot_general(q[:, cols], k_ref[:, cols], (((1,), (1,)), ((), ())), preferred_element_type=F32)
              for cols in heads]
    for cols, s in zip(heads, scores):
        m = jnp.max(s, axis=-1, keepdims=True)
        p = jnp.exp(s - m)
        l = jnp.sum(p, axis=-1, keepdims=True)
        o_ref[:, cols] = (_dot(p.astype(BF16), v_ref[:, cols]) * (1.0 / l)).astype(BF16)
    x = x + _rms(_dot(o_ref[...], wo_ref[...]), gxpost_ref[...])
    h = _rms(x, gfpre_ref[...]).astype(BF16)
    for c in range(D_FF // FF_CHUNK):
        cols = slice(c * FF_CHUNK, (c + 1) * FF_CHUNK)
        a = jnp.maximum(_dot(h, w1_ref[:, cols]), 0.0)
        a = (a * a).astype(BF16)
        part = _dot(a, w2_ref[cols, :])
        if c == 0:
            acc_ref[...] = part
        else:
            acc_ref[...] += part
    out_ref[...] = x + _rms(acc_ref[...], gfpost_ref[...])


def _xattn_ffn(x, l, k, v, gxpre, wq, wo, gxpost, gfpre, w1, w2, gfpost):
    B, S, D = x.shape
    tm = TOKEN_BLOCK
    hw = X_HEADS * X_HEAD_DIM
    xb = pl.BlockSpec((None, tm, D), lambda b, i: (b, i, 0))
    kb = pl.BlockSpec((None, N_MEM, hw), lambda b, i: (b, 0, 0))
    consts = (gxpre, wq, wo, gxpost, gfpre, w1, w2, gfpost)
    return pl.pallas_call(
        _xattn_ffn_kernel,
        grid=(B, S // tm),
        in_specs=[xb, kb, kb] + [_layer_spec(a.shape, l) if a.ndim == 3 else _const_spec(a.shape) for a in consts],
        out_specs=xb,
        out_shape=jax.ShapeDtypeStruct((B, S, D), F32),
        scratch_shapes=[pltpu.VMEM((tm, hw), BF16), pltpu.VMEM((tm, D), F32)],
        compiler_params=_params("parallel", "parallel"),
        name="xattn_ffn",
    )(x, k, v, *consts)


def _t5_bucket(n):
    exact = REL_BUCKETS // 2
    nf = np.maximum(n, 1).astype(np.float32)
    large = exact + (np.log(nf / exact) / np.log(REL_MAX_DIST / exact) * (REL_BUCKETS - exact)).astype(np.int32)
    large = np.minimum(large, REL_BUCKETS - 1)
    return np.where(n < exact, n, large).astype(np.int32)


def _band_bias(rel_bias, g, band, dil):
    d = np.arange(band + 1)
    table = rel_bias[jnp.asarray(_t5_bucket(d * dil))][:, g * ATT_HEADS:(g + 1) * ATT_HEADS].astype(F32) * LOG2E
    pad = jnp.full((ATT_HEADS, band - 1), NEG_INF, F32)
    f = jnp.concatenate([pad, table[::-1].T, pad], axis=1)
    full = jnp.stack([f[:, band - 1 - i:3 * band - 1 - i] for i in range(band)], axis=1)
    no_prev = jnp.concatenate([jnp.full((ATT_HEADS, band, band), NEG_INF, F32), full[:, :, band:]], axis=2)
    return jnp.stack([full, no_prev], axis=0)


def kernel(x, mem, rel_bias, g_mix_pre, g_mix_post, w_in, gate_b, pool_w, pool_scale, a_re, a_im, log_dt,
           b_re, b_im, c_re, c_im, d_skip, w_glu, b_glu, sgu_ln_g, sgu_ln_b, w_s, b_s, w_up, w_out,
           g_x_pre, g_x_post, g_mem, w_cq, w_ckv, w_co, g_ff_pre, g_ff_post, w_ff1, w_ff2):
    B, S, D = x.shape
    depth = w_in.shape[0]
    assert D == D_MODEL and S % (TOKEN_BLOCK) == 0 and B == V7X_SUBLANES
    for win, dil in DIL_GROUPS:
        assert win // dil == ATT_BAND and (S // dil) % ATT_BAND == 0

    biases = [_band_bias(rel_bias, g, win // dil, dil) for g, (win, dil) in enumerate(DIL_GROUPS)]
    row = lambda a: a.reshape(1, -1).astype(F32)
    w_mix, w_gate = w_in[:, :, :OFF_GATE].astype(BF16), w_in[:, :, OFF_GATE:].astype(BF16)
    w_up, w_out, w_glu, w_cq, w_ckv, w_co, w_ff1, w_ff2 = (
        a.astype(BF16) for a in (w_up, w_out, w_glu, w_cq, w_ckv, w_co, w_ff1, w_ff2))

    for l in range(depth):
        a_out, qkv1, qkv2, qkv3, ssm_in, d_out = _in_proj(
            x, l, row(g_mix_pre[l]), w_mix, pool_w[l].astype(BF16), row(pool_scale[l]),
            row(sgu_ln_g[l]), row(sgu_ln_b[l]), w_s[l], b_s[l].T)
        s5p = _s5_params(a_re[l], a_im[l], log_dt[l], b_re[l], b_im[l], c_re[l], c_im[l])
        c_out = _s5(ssm_in, l, *s5p, row(d_skip[l]), w_glu, row(b_glu[l]))
        outs, stats = zip(*[_attn_group(qkv, bias, dil)
                            for qkv, bias, (win, dil) in zip((qkv1, qkv2, qkv3), biases, DIL_GROUPS)])
        x = _merge(x, l, a_out, outs, stats, c_out, d_out, row(g_mix_pre[l]), w_gate, gate_b[l], w_up, w_out,
                   row(g_mix_post[l]))
        k_mem, v_mem = _mem_kv(mem, l, row(g_mem[l]), w_ckv)
        x = _xattn_ffn(x, l, k_mem, v_mem, row(g_x_pre[l]), w_cq, w_co, row(g_x_post[l]),
                       row(g_ff_pre[l]), w_ff1, w_ff2, row(g_ff_post[l]))
    return x
```
